```python
import math
import jax
import jax.numpy as jnp
from jax import lax
import numpy as np

D_MODEL = 2048
BATCH = 2
SEQ = 16384
DEPTH = 2
DEC_BATCH = 32
DEC_SEQ = 64
PAST_LEN = 4096

CHUNK = 64
N_EVEN = (DEPTH + 1) // 2
N_ODD = DEPTH // 2
D_FF = 5632
ALPHA = (2 * DEPTH) ** 0.25
BETA = (8 * DEPTH) ** -0.25
LN_EPS = 1e-5
RMS_EPS = 1e-6
POOL_WINDOWS = (2, 4, 8, 16)
POOL_GROUPS = len(POOL_WINDOWS)
POOL_CH = 384
POOL_WIDTH = POOL_GROUPS * POOL_CH
POOL_HIST = max(POOL_WINDOWS) - 1
SSM_WIDTH = D_MODEL - POOL_WIDTH
SSM_GROUP_CH = 16
SSM_GROUPS = SSM_WIDTH // SSM_GROUP_CH
SSM_STATE = 64
DT_MIN = 0.001
DT_MAX = 0.1
MLA_HEADS = 16
Q_LORA = 512
KV_LORA = 256
QK_NOPE = 64
QK_ROPE = 32
V_HEAD = 64
MLA_WIDTH = MLA_HEADS * V_HEAD
MLA_IN = Q_LORA + KV_LORA + QK_ROPE
ATTN_SCALE = (QK_NOPE + QK_ROPE) ** -0.5
ROPE_THETA = 10000.0
Q_BLOCK = 128
SG_CHUNK = 128
SG_GROUPS = 8
SG_WIDTH = D_MODEL - MLA_WIDTH
SG_CH = SG_WIDTH // SG_GROUPS
ODD_IN = MLA_IN + 2 * SG_WIDTH
NEG_INF = -1e30

kernel_name = 'hybrid_streaming_encoder_step'


def layer_norm(x, g, b):
    xf = x.astype(jnp.float32)
    mu = jnp.mean(xf, axis=-1, keepdims=True)
    var = jnp.mean(jnp.square(xf - mu), axis=-1, keepdims=True)
    return ((xf - mu) * lax.rsqrt(var + LN_EPS)).astype(x.dtype) * g + b


def rms_norm(x, g):
    xf = x.astype(jnp.float32)
    return (xf * lax.rsqrt(jnp.mean(xf * xf, axis=-1, keepdims=True) + RMS_EPS)).astype(x.dtype) * g


def swiglu_ffn(x, w1, w3, w2):
    return (jax.nn.silu(x @ w1) * (x @ w3)) @ w2


def rotary(x, pos):
    half = QK_ROPE // 2
    inv = ROPE_THETA ** (-jnp.arange(half, dtype=jnp.float32) / half)
    ang = pos.astype(jnp.float32)[:, None] * inv[None, :]
    shape = (1, x.shape[1]) + (1,) * (x.ndim - 3) + (half,)
    cos = jnp.cos(ang).reshape(shape)
    sin = jnp.sin(ang).reshape(shape)
    xf = x.astype(jnp.float32)
    x1, x2 = xf[..., :half], xf[..., half:]
    return jnp.concatenate([x1 * cos - x2 * sin, x2 * cos + x1 * sin], axis=-1).astype(x.dtype)


def pool_mixer(u, hist, pos0, w_pool, pool_scale):
    nb, s, _ = u.shape
    full = jnp.concatenate([hist, u], axis=1)
    cs = jnp.cumsum(full.astype(jnp.float32), axis=1)
    cs = jnp.concatenate([jnp.zeros((nb, 1, POOL_WIDTH), jnp.float32), cs], axis=1)
    t = jnp.arange(s)
    means = []
    for g, w in enumerate(POOL_WINDOWS):
        sl = slice(g * POOL_CH, (g + 1) * POOL_CH)
        hi = cs[:, POOL_HIST + 1:POOL_HIST + 1 + s, sl]
        lo = cs[:, POOL_HIST + 1 - w:POOL_HIST + 1 - w + s, sl]
        cnt = jnp.minimum(pos0 + t + 1, w).astype(jnp.float32)
        means.append((hi - lo) / cnt[None, :, None])
    mean = jnp.stack(means, axis=2).astype(u.dtype)
    d = mean - u.reshape(nb, s, POOL_GROUPS, POOL_CH)
    out = jnp.einsum('bsgc,gcd->bsgd', d, w_pool) * pool_scale
    return out.reshape(nb, s, POOL_WIDTH), full[:, -POOL_HIST:]


def _ssm_combine(e1, e2):
    ar, ai, br, bi = e1
    cr, ci, dr, di = e2
    return (cr * ar - ci * ai, cr * ai + ci * ar,
            cr * br - ci * bi + dr, cr * bi + ci * br + di)


def s5_mixer(u, h_re, h_im, lam_re, lam_im, log_dt, b_re, b_im, c_re, c_im, d_skip, w_glu, b_glu):
    nb, s, _ = u.shape
    f32 = jnp.float32
    uf = u.astype(f32).reshape(nb, s, SSM_GROUPS, SSM_GROUP_CH)
    dt = jnp.exp(log_dt.astype(f32))[:, None]
    lr = lam_re.astype(f32)
    li = lam_im.astype(f32)
    mag = jnp.exp(lr * dt)
    ab_re = mag * jnp.cos(li * dt)
    ab_im = mag * jnp.sin(li * dt)
    den = lr * lr + li * li
    nr = ab_re - 1.0
    co_re = (nr * lr + ab_im * li) / den
    co_im = (ab_im * lr - nr * li) / den
    br = b_re.astype(f32)
    bi = b_im.astype(f32)
    bb_re = co_re[..., None] * br - co_im[..., None] * bi
    bb_im = co_re[..., None] * bi + co_im[..., None] * br
    bu_re = jnp.einsum('bsgc,gpc->bsgp', uf, bb_re)
    bu_im = jnp.einsum('bsgc,gpc->bsgp', uf, bb_im)
    if h_re is not None:
        hr = h_re.astype(f32)
        hi = h_im.astype(f32)
        bu_re = bu_re.at[:, 0].add(ab_re * hr - ab_im * hi)
        bu_im = bu_im.at[:, 0].add(ab_re * hi + ab_im * hr)
    a_re = jnp.broadcast_to(ab_re, bu_re.shape)
    a_im = jnp.broadcast_to(ab_im, bu_im.shape)
    _, _, x_re, x_im = lax.associative_scan(_ssm_combine, (a_re, a_im, bu_re, bu_im), axis=1)
    y = (jnp.einsum('bsgp,gcp->bsgc', x_re, c_re.astype(f32))
         - jnp.einsum('bsgp,gcp->bsgc', x_im, c_im.astype(f32))
         + d_skip.astype(f32).reshape(SSM_GROUPS, SSM_GROUP_CH) * uf)
    y = y.reshape(nb, s, SSM_WIDTH).astype(u.dtype)
    g = jax.nn.gelu(y)
    out = g * jax.nn.sigmoid(g @ w_glu + b_glu)
    return out, x_re[:, -1].astype(u.dtype), x_im[:, -1].astype(u.dtype)


def mla_project(z, pos, g_q, g_kv, w_uq):
    cq = z[..., :Q_LORA]
    ckv = z[..., Q_LORA:Q_LORA + KV_LORA]
    kpe = z[..., Q_LORA + KV_LORA:MLA_IN]
    q = jnp.einsum('bsr,rhd->bshd', rms_norm(cq, g_q), w_uq)
    q_nope = q[..., :QK_NOPE]
    q_pe = rotary(q[..., QK_NOPE:], pos)
    c_kv = rms_norm(ckv, g_kv)
    k_pe = rotary(kpe, pos)
    return q_nope, q_pe, c_kv, k_pe


def mla_attend_prompt(q_nope, q_pe, c_kv, k_pe, w_uk, w_uv):
    nb, s = q_nope.shape[:2]
    k_nope = jnp.einsum('btr,rhd->bthd', c_kv, w_uk)
    v = jnp.einsum('btr,rhd->bthd', c_kv, w_uv)
    key_chunk = jnp.arange(s) // CHUNK
    nblk = s // Q_BLOCK
    qn = q_nope.reshape(nb, nblk, Q_BLOCK, MLA_HEADS, QK_NOPE).transpose(1, 0, 2, 3, 4)
    qp = q_pe.reshape(nb, nblk, Q_BLOCK, MLA_HEADS, QK_ROPE).transpose(1, 0, 2, 3, 4)

    def block(args):
        qn_b, qp_b, i = args
        sc = (jnp.einsum('bqhd,bthd->bhqt', qn_b, k_nope)
              + jnp.einsum('bqhd,btd->bhqt', qp_b, k_pe)).astype(jnp.float32) * ATTN_SCALE
        q_chunk = (i * Q_BLOCK + jnp.arange(Q_BLOCK)) // CHUNK
        mask = key_chunk[None, :] <= q_chunk[:, None]
        p = jax.nn.softmax(jnp.where(mask, sc, NEG_INF), axis=-1).astype(v.dtype)
        return jnp.einsum('bhqt,bthd->bqhd', p, v)

    out = lax.map(block, (qn, qp, jnp.arange(nblk)))
    return out.transpose(1, 0, 2, 3, 4).reshape(nb, s, MLA_WIDTH)


def mla_attend_sample(q_nope, q_pe, c_all, kpe_all, q_pos, k_pos, w_uk, w_uv):
    nb, s = q_nope.shape[:2]
    q_lat = jnp.einsum('bshd,rhd->bshr', q_nope, w_uk)
    sc = (jnp.einsum('bshr,btr->bhst', q_lat, c_all)
          + jnp.einsum('bshd,btd->bhst', q_pe, kpe_all)).astype(jnp.float32) * ATTN_SCALE
    mask = (k_pos // CHUNK)[None, :] <= (q_pos // CHUNK)[:, None]
    p = jax.nn.softmax(jnp.where(mask, sc, NEG_INF), axis=-1).astype(c_all.dtype)
    o_lat = jnp.einsum('bhst,btr->bshr', p, c_all)
    out = jnp.einsum('bshr,rhd->bshd', o_lat, w_uv)
    return out.reshape(nb, s, MLA_WIDTH)


def sgu_mixer(z, g_v, b_v, w_s, b_s):
    nb, s, _ = z.shape
    u = z[..., :SG_WIDTH]
    v = layer_norm(z[..., SG_WIDTH:], g_v, b_v)
    L = min(s, SG_CHUNK)
    vc = v.reshape(nb, s // L, L, SG_GROUPS, SG_CH)
    w = w_s[:, :L, :L] * jnp.tril(jnp.ones((L, L), w_s.dtype))
    mixed = jnp.einsum('gts,bnsgc->bntgc', w, vc) + b_s[:, :L].T[None, None, :, :, None]
    return u * mixed.reshape(nb, s, SG_WIDTH), v


def trunk(x, past_len, pool_hist, ssm_re, ssm_im, ckv_cache, kpe_cache, P):
    nb, s, _ = x.shape
    pos = past_len + jnp.arange(s, dtype=jnp.int32)
    pools, sres, sims, ckvs, kpes, sgvs = [], [], [], [], [], []
    for layer in range(DEPTH):
        ffn1 = swiglu_ffn(x, P['ffn1_w1'][layer], P['ffn1_w3'][layer], P['ffn1_w2'][layer])
        x = layer_norm(ALPHA * x + 0.5 * ffn1, P['ln_g'][layer, 0], P['ln_b'][layer, 0])
        i = layer // 2
        if layer % 2 == 0:
            z = x @ P['w_in_e'][i]
            hist = jnp.zeros((nb, POOL_HIST, POOL_WIDTH), x.dtype) if pool_hist is None else pool_hist[i]
            a_out, new_hist = pool_mixer(z[..., :POOL_WIDTH], hist, past_len, P['pool_w'][i], P['pool_scale'][i])
            h_re = None if ssm_re is None else ssm_re[i]
            h_im = None if ssm_im is None else ssm_im[i]
            b_out, s_re, s_im = s5_mixer(z[..., POOL_WIDTH:], h_re, h_im, P['ssm_lam_re'][i], P['ssm_lam_im'][i],
                                         P['ssm_log_dt'][i], P['ssm_b_re'][i], P['ssm_b_im'][i], P['ssm_c_re'][i],
                                         P['ssm_c_im'][i], P['ssm_d'][i], P['ssm_w_glu'][i], P['ssm_b_glu'][i])
            mix = jnp.concatenate([a_out, b_out], axis=-1) @ P['w_out_e'][i]
            pools.append(new_hist)
            sres.append(s_re)
            sims.append(s_im)
        else:
            z = x @ P['w_in_o'][i]
            q_nope, q_pe, c_kv, k_pe = mla_project(z, pos, P['mla_g_q'][i], P['mla_g_kv'][i], P['mla_w_uq'][i])
            if ckv_cache is None:
                att = mla_attend_prompt(q_nope, q_pe, c_kv, k_pe, P['mla_w_uk'][i], P['mla_w_uv'][i])
            else:
                c_all = jnp.concatenate([ckv_cache[i], c_kv], axis=1)
                kpe_all = jnp.concatenate([kpe_cache[i], k_pe], axis=1)
                k_pos = jnp.arange(c_all.shape[1], dtype=jnp.int32)
                att = mla_attend_sample(q_nope, q_pe, c_all, kpe_all, pos, k_pos, P['mla_w_uk'][i], P['mla_w_uv'][i])
            sg_out, v_rows = sgu_mixer(z[..., MLA_IN:], P['sg_g_v'][i], P['sg_b_v'][i], P['sg_w_s'][i], P['sg_b_s'][i])
            mix = jnp.concatenate([att, sg_out], axis=-1) @ P['w_out_o'][i]
            ckvs.append(c_kv)
            kpes.append(k_pe)
            sgvs.append(v_rows)
        x = layer_norm(ALPHA * x + mix, P['ln_g'][layer, 1], P['ln_b'][layer, 1])
        ffn2 = swiglu_ffn(x, P['ffn2_w1'][layer], P['ffn2_w3'][layer], P['ffn2_w2'][layer])
        x = layer_norm(ALPHA * x + 0.5 * ffn2, P['ln_g'][layer, 2], P['ln_b'][layer, 2])
    return (x, jnp.stack(pools), jnp.stack(sres), jnp.stack(sims),
            jnp.stack(ckvs), jnp.stack(kpes), jnp.stack(sgvs))


def setup_inputs(seed: int = 0) -> dict:
    key = jax.random.key(seed)
    ks = iter(jax.random.split(key, 64))
    f32 = jnp.float32

    def nrm(shape, scale):
        return jax.random.normal(next(ks), shape, f32) * scale

    n_idx = jnp.arange(SSM_STATE, dtype=f32)
    inp = {}
    inp['x_prompt'] = nrm((BATCH, SEQ, D_MODEL), 1.0)
    inp['x_sample'] = nrm((DEC_BATCH, DEC_SEQ, D_MODEL), 1.0)
    inp['cache_pool'] = nrm((N_EVEN, DEC_BATCH, POOL_HIST, POOL_WIDTH), 1.0)
    inp['state_ssm_re'] = nrm((N_EVEN, DEC_BATCH, SSM_GROUPS, SSM_STATE), 0.1)
    inp['state_ssm_im'] = nrm((N_EVEN, DEC_BATCH, SSM_GROUPS, SSM_STATE), 0.1)
    inp['cache_ckv'] = nrm((N_ODD, DEC_BATCH, PAST_LEN, KV_LORA), 1.0)
    inp['cache_kpe'] = nrm((N_ODD, DEC_BATCH, PAST_LEN, QK_ROPE), 1.0)
    inp['ln_g'] = 1.0 + nrm((DEPTH, 3, D_MODEL), 0.02)
    inp['ln_b'] = nrm((DEPTH, 3, D_MODEL), 0.02)
    inp['ffn1_w1'] = nrm((DEPTH, D_MODEL, D_FF), D_MODEL ** -0.5)
    inp['ffn1_w3'] = nrm((DEPTH, D_MODEL, D_FF), D_MODEL ** -0.5)
    inp['ffn1_w2'] = nrm((DEPTH, D_FF, D_MODEL), BETA * D_FF ** -0.5)
    inp['ffn2_w1'] = nrm((DEPTH, D_MODEL, D_FF), D_MODEL ** -0.5)
    inp['ffn2_w3'] = nrm((DEPTH, D_MODEL, D_FF), D_MODEL ** -0.5)
    inp['ffn2_w2'] = nrm((DEPTH, D_FF, D_MODEL), BETA * D_FF ** -0.5)
    inp['w_in_e'] = nrm((N_EVEN, D_MODEL, POOL_WIDTH + SSM_WIDTH), D_MODEL ** -0.5)
    inp['pool_w'] = nrm((N_EVEN, POOL_GROUPS, POOL_CH, POOL_CH), POOL_CH ** -0.5)
    inp['pool_scale'] = 1.0 + nrm((N_EVEN, POOL_GROUPS, POOL_CH), 0.02)
    inp['ssm_lam_re'] = -0.5 + nrm((N_EVEN, SSM_GROUPS, SSM_STATE), 0.01)
    inp['ssm_lam_im'] = math.pi * n_idx + nrm((N_EVEN, SSM_GROUPS, SSM_STATE), 0.01)
    inp['ssm_log_dt'] = jax.random.uniform(next(ks), (N_EVEN, SSM_GROUPS), f32,
                                           minval=math.log(DT_MIN), maxval=math.log(DT_MAX))
    inp['ssm_b_re'] = nrm((N_EVEN, SSM_GROUPS, SSM_STATE, SSM_GROUP_CH), (2 * SSM_GROUP_CH) ** -0.5)
    inp['ssm_b_im'] = nrm((N_EVEN, SSM_GROUPS, SSM_STATE, SSM_GROUP_CH), (2 * SSM_GROUP_CH) ** -0.5)
    inp['ssm_c_re'] = nrm((N_EVEN, SSM_GROUPS, SSM_GROUP_CH, SSM_STATE), (2 * SSM_STATE) ** -0.5)
    inp['ssm_c_im'] = nrm((N_EVEN, SSM_GROUPS, SSM_GROUP_CH, SSM_STATE), (2 * SSM_STATE) ** -0.5)
    inp['ssm_d'] = nrm((N_EVEN, SSM_WIDTH), 1.0)
    inp['ssm_w_glu'] = nrm((N_EVEN, SSM_WIDTH, SSM_WIDTH), SSM_WIDTH ** -0.5)
    inp['ssm_b_glu'] = nrm((N_EVEN, SSM_WIDTH), 0.02)
    inp['w_out_e'] = nrm((N_EVEN, POOL_WIDTH + SSM_WIDTH, D_MODEL), BETA * D_MODEL ** -0.5)
    inp['w_in_o'] = nrm((N_ODD, D_MODEL, ODD_IN), D_MODEL ** -0.5)
    inp['mla_g_q'] = 1.0 + nrm((N_ODD, Q_LORA), 0.02)
    inp['mla_g_kv'] = 1.0 + nrm((N_ODD, KV_LORA), 0.02)
    inp['mla_w_uq'] = nrm((N_ODD, Q_LORA, MLA_HEADS, QK_NOPE + QK_ROPE), Q_LORA ** -0.5)
    inp['mla_w_uk'] = nrm((N_ODD, KV_LORA, MLA_HEADS, QK_NOPE), KV_LORA ** -0.5)
    inp['mla_w_uv'] = nrm((N_ODD, KV_LORA, MLA_HEADS, V_HEAD), BETA * KV_LORA ** -0.5)
    inp['sg_g_v'] = 1.0 + nrm((N_ODD, SG_WIDTH), 0.02)
    inp['sg_b_v'] = nrm((N_ODD, SG_WIDTH), 0.02)
    inp['sg_w_s'] = nrm((N_ODD, SG_GROUPS, SG_CHUNK, SG_CHUNK), 0.5 * SG_CHUNK ** -0.5)
    inp['sg_b_s'] = 1.0 + nrm((N_ODD, SG_GROUPS, SG_CHUNK), 0.02)
    inp['w_out_o'] = nrm((N_ODD, MLA_WIDTH + SG_WIDTH, D_MODEL), BETA * D_MODEL ** -0.5)
    return inp


def reference(x_prompt, x_sample, cache_pool, state_ssm_re, state_ssm_im, cache_ckv, cache_kpe,
              ln_g, ln_b, ffn1_w1, ffn1_w3, ffn1_w2, ffn2_w1, ffn2_w3, ffn2_w2,
              w_in_e, pool_w, pool_scale, ssm_lam_re, ssm_lam_im, ssm_log_dt, ssm_b_re, ssm_b_im,
              ssm_c_re, ssm_c_im, ssm_d, ssm_w_glu, ssm_b_glu, w_out_e,
              w_in_o, mla_g_q, mla_g_kv, mla_w_uq, mla_w_uk, mla_w_uv,
              sg_g_v, sg_b_v, sg_w_s, sg_b_s, w_out_o):
    P = dict(ln_g=ln_g, ln_b=ln_b, ffn1_w1=ffn1_w1, ffn1_w3=ffn1_w3, ffn1_w2=ffn1_w2,
             ffn2_w1=ffn2_w1, ffn2_w3=ffn2_w3, ffn2_w2=ffn2_w2,
             w_in_e=w_in_e, pool_w=pool_w, pool_scale=pool_scale, ssm_lam_re=ssm_lam_re,
             ssm_lam_im=ssm_lam_im, ssm_log_dt=ssm_log_dt, ssm_b_re=ssm_b_re, ssm_b_im=ssm_b_im,
             ssm_c_re=ssm_c_re, ssm_c_im=ssm_c_im, ssm_d=ssm_d, ssm_w_glu=ssm_w_glu,
             ssm_b_glu=ssm_b_glu, w_out_e=w_out_e,
             w_in_o=w_in_o, mla_g_q=mla_g_q, mla_g_kv=mla_g_kv, mla_w_uq=mla_w_uq,
             mla_w_uk=mla_w_uk, mla_w_uv=mla_w_uv, sg_g_v=sg_g_v, sg_b_v=sg_b_v,
             sg_w_s=sg_w_s, sg_b_s=sg_b_s, w_out_o=w_out_o)
    y_prompt, pool_p, sre_p, sim_p, ckv_p, kpe_p, _ = trunk(
        x_prompt, 0, None, None, None, None, None, P)
    past_len = cache_ckv.shape[2]
    y_sample, pool_s, sre_s, sim_s, ckv_s, kpe_s, sgv_s = trunk(
        x_sample, past_len, cache_pool, state_ssm_re, state_ssm_im, cache_ckv, cache_kpe, P)
    return (y_prompt, y_sample, pool_p, sre_p, sim_p, ckv_p, kpe_p,
            pool_s, sre_s, sim_s, ckv_s, kpe_s, sgv_s)
```

```python
import functools
import math

import jax
import jax.numpy as jnp
from jax import lax
from jax.experimental import pallas as pl
from jax.experimental.pallas import tpu as pltpu

F32 = jnp.float32
BF16 = jnp.bfloat16

D_MODEL = 2048
DEPTH = 2
CHUNK = 64
D_FF = 5632
ALPHA = (2 * DEPTH) ** 0.25
LN_EPS = 1e-5
RMS_EPS = 1e-6
POOL_WINDOWS = (2, 4, 8, 16)
POOL_CH = 384
POOL_WIDTH = len(POOL_WINDOWS) * POOL_CH
POOL_HIST = max(POOL_WINDOWS) - 1
POOL_HALO = 16
SSM_WIDTH = D_MODEL - POOL_WIDTH
SSM_GROUP_CH = 16
SSM_GROUPS = SSM_WIDTH // SSM_GROUP_CH
SSM_STATE = 64
SSM_LANES = SSM_GROUPS * SSM_STATE
MLA_HEADS = 16
Q_LORA = 512
KV_LORA = 256
QK_NOPE = 64
QK_ROPE = 32
V_HEAD = 64
HEAD_PAD = 128
MLA_WIDTH = MLA_HEADS * V_HEAD
MLA_IN = Q_LORA + KV_LORA + QK_ROPE
ATTN_SCALE = (QK_NOPE + QK_ROPE) ** -0.5
ROPE_THETA = 10000.0
SG_CHUNK = 128
SG_GROUPS = 8
SG_WIDTH = D_MODEL - MLA_WIDTH
SG_CH = SG_WIDTH // SG_GROUPS
NEG_INF = -1e30

VMEM_LIMIT_BYTES = 56 * 1024 * 1024


def _params(*semantics):
    return pltpu.CompilerParams(dimension_semantics=semantics, vmem_limit_bytes=VMEM_LIMIT_BYTES)


def _dot(a, b):
    return jnp.dot(a, b, preferred_element_type=F32)


def _dot_nt(a, b):
    return lax.dot_general(a, b, (((1,), (1,)), ((), ())), preferred_element_type=F32)


def _layer_norm(y, g, b):
    mu = jnp.mean(y, axis=-1, keepdims=True)
    d = y - mu
    var = jnp.mean(d * d, axis=-1, keepdims=True)
    return d * lax.rsqrt(var + LN_EPS) * g + b


def _rms_norm(y, g):
    return y * lax.rsqrt(jnp.mean(y * y, axis=-1, keepdims=True) + RMS_EPS) * g


def _sigmoid(y):
    return 1.0 / (1.0 + jnp.exp(-y))


def _gelu_tanh(y):
    c = math.sqrt(2.0 / math.pi)
    return 0.5 * y * (1.0 + jnp.tanh(c * (y + 0.044715 * (y * y * y))))


def _ffn_kernel(x_ref, w1_ref, w3_ref, w2_ref, g_ref, b_ref, o_ref, xb_ref, *, nf):
    f = pl.program_id(1)

    @pl.when(f == 0)
    def _():
        xb_ref[...] = x_ref[...].astype(BF16)

    xb = xb_ref[...]
    h1 = _dot(xb, w1_ref[...])
    h3 = _dot(xb, w3_ref[...])
    h = (h1 * _sigmoid(h1) * h3).astype(BF16)
    c = _dot(h, w2_ref[...])

    @pl.when(f == 0)
    def _():
        o_ref[...] = c

    @pl.when(f > 0)
    def _():
        o_ref[...] += c

    @pl.when(f == nf - 1)
    def _():
        y = ALPHA * x_ref[...] + 0.5 * o_ref[...]
        o_ref[...] = _layer_norm(y, g_ref[...], b_ref[...])


def _ffn(x, w1, w3, w2, g, b, *, tm, tf):
    n, d = x.shape
    nf = w1.shape[1] // tf
    return pl.pallas_call(
        functools.partial(_ffn_kernel, nf=nf),
        grid=(n // tm, nf),
        in_specs=[
            pl.BlockSpec((tm, d), lambda i, j: (i, 0)),
            pl.BlockSpec((d, tf), lambda i, j: (0, j)),
            pl.BlockSpec((d, tf), lambda i, j: (0, j)),
            pl.BlockSpec((tf, d), lambda i, j: (j, 0)),
            pl.BlockSpec((1, d), lambda i, j: (0, 0)),
            pl.BlockSpec((1, d), lambda i, j: (0, 0)),
        ],
        out_specs=pl.BlockSpec((tm, d), lambda i, j: (i, 0)),
        out_shape=jax.ShapeDtypeStruct((n, d), F32),
        scratch_shapes=[pltpu.VMEM((tm, d), BF16)],
        compiler_params=_params("parallel", "arbitrary"),
        name="ffn_ln",
    )(x, w1, w3, w2, g, b)


def _proj_kernel(x_ref, w_ref, o_ref):
    o_ref[...] = _dot(x_ref[...].astype(BF16), w_ref[...])


def _proj(x, w, *, tm):
    n, d = x.shape
    m = w.shape[1]
    return pl.pallas_call(
        _proj_kernel,
        grid=(n // tm,),
        in_specs=[pl.BlockSpec((tm, d), lambda i: (i, 0)), pl.BlockSpec((d, m), lambda i: (0, 0))],
        out_specs=pl.BlockSpec((tm, m), lambda i: (i, 0)),
        out_shape=jax.ShapeDtypeStruct((n, m), F32),
        compiler_params=_params("parallel"),
        name="proj_in",
    )(x, w)


def _out_ln_kernel(x_ref, a_ref, b_ref, wa_ref, wb_ref, g_ref, beta_ref, o_ref):
    mix = _dot(a_ref[...], wa_ref[...]) + _dot(b_ref[...], wb_ref[...])
    o_ref[...] = _layer_norm(ALPHA * x_ref[...] + mix, g_ref[...], beta_ref[...])


def _out_ln(x, a, b, wa, wb, g, beta, *, tm):
    n, d = x.shape
    ka, kb = a.shape[1], b.shape[1]
    return pl.pallas_call(
        _out_ln_kernel,
        grid=(n // tm,),
        in_specs=[
            pl.BlockSpec((tm, d), lambda i: (i, 0)),
            pl.BlockSpec((tm, ka), lambda i: (i, 0)),
            pl.BlockSpec((tm, kb), lambda i: (i, 0)),
            pl.BlockSpec((ka, d), lambda i: (0, 0)),
            pl.BlockSpec((kb, d), lambda i: (0, 0)),
            pl.BlockSpec((1, d), lambda i: (0, 0)),
            pl.BlockSpec((1, d), lambda i: (0, 0)),
        ],
        out_specs=pl.BlockSpec((tm, d), lambda i: (i, 0)),
        out_shape=jax.ShapeDtypeStruct((n, d), F32),
        compiler_params=_params("parallel"),
        name="out_ln",
    )(x, a, b, wa, wb, g, beta)


def _pool_kernel(u_ref, halo_ref, w_ref, sc_ref, o_ref, buf_ref, *, sb, t, pos0, zero_first):
    ti = pl.program_id(1)
    pos = lax.broadcasted_iota(jnp.int32, (t, 1), 0) + (ti * t + pos0 + 1)
    for s in range(sb):
        halo = halo_ref[s]
        if zero_first:
            halo = jnp.where(ti == 0, 0.0, halo)
        buf_ref[0:POOL_HALO, :] = halo
        buf_ref[POOL_HALO:POOL_HALO + t, :] = u_ref[s]
        for g, w in enumerate(POOL_WINDOWS):
            sl = slice(g * POOL_CH, (g + 1) * POOL_CH)
            u = buf_ref[POOL_HALO:POOL_HALO + t, sl]
            acc = u
            for k in range(1, w):
                acc = acc + buf_ref[POOL_HALO - k:POOL_HALO - k + t, sl]
            cnt = jnp.minimum(pos, w).astype(F32)
            d = (acc / cnt - u).astype(BF16)
            o_ref[s, :, sl] = (_dot(d, w_ref[g]) * sc_ref[g]).astype(o_ref.dtype)


def _pool(z3, halo_src, w, sc, *, sb, t, pos0, zero_first):
    n_seq, s_len, _ = z3.shape
    if zero_first:
        hb = t // POOL_HALO
        halo_map = lambda i, j: (i, jnp.maximum(j * hb - 1, 0), 0)
    else:
        halo_map = lambda i, j: (i, 0, 0)
    return pl.pallas_call(
        functools.partial(_pool_kernel, sb=sb, t=t, pos0=pos0, zero_first=zero_first),
        grid=(n_seq // sb, s_len // t),
        in_specs=[
            pl.BlockSpec((sb, t, POOL_WIDTH), lambda i, j: (i, j, 0)),
            pl.BlockSpec((sb, POOL_HALO, POOL_WIDTH), halo_map),
            pl.BlockSpec((len(POOL_WINDOWS), POOL_CH, POOL_CH), lambda i, j: (0, 0, 0)),
            pl.BlockSpec((len(POOL_WINDOWS), 1, POOL_CH), lambda i, j: (0, 0, 0)),
        ],
        out_specs=pl.BlockSpec((sb, t, POOL_WIDTH), lambda i, j: (i, j, 0)),
        out_shape=jax.ShapeDtypeStruct((n_seq, s_len, POOL_WIDTH), BF16),
        scratch_shapes=[pltpu.VMEM((POOL_HALO + t, POOL_WIDTH), F32)],
        compiler_params=_params("parallel", "arbitrary"),
        name="pool_mixer",
    )(z3, halo_src, w, sc)


def _ssm_prep_kernel(lr_ref, li_ref, ldt_ref, br_ref, bi_ref, abr_ref, abi_ref, bbr_ref, bbi_ref):
    lr = lr_ref[...]
    li = li_ref[...]
    dt = jnp.exp(ldt_ref[...])
    mag = jnp.exp(lr * dt)
    abr = mag * jnp.cos(li * dt)
    abi = mag * jnp.sin(li * dt)
    den = lr * lr + li * li
    nr = abr - 1.0
    cor = (nr * lr + abi * li) / den
    coi = (abi * lr - nr * li) / den
    br = br_ref[...]
    bi = bi_ref[...]
    abr_ref[...] = abr
    abi_ref[...] = abi
    bbr_ref[...] = cor * br - coi * bi
    bbi_ref[...] = cor * bi + coi * br


def _ssm_prep(lam_re, lam_im, log_dt, b_re, b_im):
    g, p = lam_re.shape
    c = b_re.shape[-1]
    lr = lam_re.reshape(g, 1, p)
    li = lam_im.reshape(g, 1, p)
    ldt = jnp.broadcast_to(log_dt.reshape(g, 1, 1), (g, 1, p))
    brt = b_re.transpose(0, 2, 1)
    bit = b_im.transpose(0, 2, 1)
    return pl.pallas_call(
        _ssm_prep_kernel,
        out_shape=[jax.ShapeDtypeStruct((g, 1, p), F32), jax.ShapeDtypeStruct((g, 1, p), F32),
                   jax.ShapeDtypeStruct((g, c, p), F32), jax.ShapeDtypeStruct((g, c, p), F32)],
        name="ssm_discretise",
    )(lr, li, ldt, brt, bit)


SSM_LANE_SPLIT = 2


def _ssm_kernel(u_ref, hre_ref, him_ref, bbr_ref, bbi_ref, ar_ref, ai_ref, cr_ref, ci_ref, d_ref,
                wg_ref, bg_ref, o_ref, sre_ref, sim_ref, xr_ref, xi_ref, st_ref, *, t, nt):
    ti = pl.program_id(1)

    @pl.when(ti == 0)
    def _():
        st_ref[0:1, :] = hre_ref[0]
        st_ref[1:2, :] = him_ref[0]

    u = u_ref[0]
    ub = u.astype(BF16)
    xr_ref[...] = _dot(ub, bbr_ref[...])
    xi_ref[...] = _dot(ub, bbi_ref[...])

    slab = SSM_LANES // SSM_LANE_SPLIT
    for c in range(SSM_LANE_SPLIT):
        sl = slice(c * slab, (c + 1) * slab)
        ar = ar_ref[:, sl]
        ai = ai_ref[:, sl]

        def step(i, carry, sl=sl, ar=ar, ai=ai):
            pr, pi = carry
            nr = ar * pr - ai * pi + xr_ref[pl.ds(i, 1), sl]
            ni = ar * pi + ai * pr + xi_ref[pl.ds(i, 1), sl]
            xr_ref[pl.ds(i, 1), sl] = nr
            xi_ref[pl.ds(i, 1), sl] = ni
            return nr, ni

        pr, pi = lax.fori_loop(0, t, step, (st_ref[0:1, sl], st_ref[1:2, sl]))
        st_ref[0:1, sl] = pr
        st_ref[1:2, sl] = pi

    y = (_dot(xr_ref[...].astype(BF16), cr_ref[...]) - _dot(xi_ref[...].astype(BF16), ci_ref[...])
         + d_ref[...] * u)
    gl = _gelu_tanh(y)
    o_ref[0] = (gl * _sigmoid(_dot(gl.astype(BF16), wg_ref[...]) + bg_ref[...])).astype(o_ref.dtype)

    @pl.when(ti == nt - 1)
    def _():
        sre_ref[0] = st_ref[0:1, :]
        sim_ref[0] = st_ref[1:2, :]


def _ssm(z3, h_re, h_im, bd_b_re, bd_b_im, a_re, a_im, bd_c_re, bd_c_im, d_skip, w_glu, b_glu, *, t):
    n_seq, s_len, _ = z3.shape
    nt = s_len // t
    col = POOL_WIDTH // SSM_WIDTH
    const2 = lambda i, j: (0, 0)
    state_spec = pl.BlockSpec((1, 1, SSM_LANES), lambda i, j: (i, 0, 0))
    return pl.pallas_call(
        functools.partial(_ssm_kernel, t=t, nt=nt),
        grid=(n_seq, nt),
        in_specs=[
            pl.BlockSpec((1, t, SSM_WIDTH), lambda i, j: (i, j, col)),
            state_spec, state_spec,
            pl.BlockSpec((SSM_WIDTH, SSM_LANES), const2),
            pl.BlockSpec((SSM_WIDTH, SSM_LANES), const2),
            pl.BlockSpec((1, SSM_LANES), const2),
            pl.BlockSpec((1, SSM_LANES), const2),
            pl.BlockSpec((SSM_LANES, SSM_WIDTH), const2),
            pl.BlockSpec((SSM_LANES, SSM_WIDTH), const2),
            pl.BlockSpec((1, SSM_WIDTH), const2),
            pl.BlockSpec((SSM_WIDTH, SSM_WIDTH), const2),
            pl.BlockSpec((1, SSM_WIDTH), const2),
        ],
        out_specs=[pl.BlockSpec((1, t, SSM_WIDTH), lambda i, j: (i, j, 0)), state_spec, state_spec],
        out_shape=[jax.ShapeDtypeStruct((n_seq, s_len, SSM_WIDTH), BF16),
                   jax.ShapeDtypeStruct((n_seq, 1, SSM_LANES), F32),
                   jax.ShapeDtypeStruct((n_seq, 1, SSM_LANES), F32)],
        scratch_shapes=[pltpu.VMEM((t, SSM_LANES), F32), pltpu.VMEM((t, SSM_LANES), F32),
                        pltpu.VMEM((8, SSM_LANES), F32)],
        compiler_params=_params("parallel", "arbitrary"),
        name="s5_mixer",
    )(z3, h_re, h_im, bd_b_re, bd_b_im, a_re, a_im, bd_c_re, bd_c_im, d_skip, w_glu, b_glu)


def _block_diag(blocks):
    g, r, c = blocks.shape
    eye = jnp.eye(g, dtype=blocks.dtype)
    return (blocks[:, :, None, :] * eye[:, None, :, None]).reshape(g * r, g * c)


def _mla_in_kernel(x_ref, cos_ref, sin_ref, wcq_ref, wckv_ref, wkpe_ref, wkpes_ref, gq_ref, gkv_ref,
                   wq_ref, wqs_ref, wk_ref, wv_ref, q_ref, k_ref, v_ref, ckv_ref, kpe_ref):
    xb = x_ref[...].astype(BF16)
    cos = cos_ref[...]
    sin = sin_ref[...]
    cqn = _rms_norm(_dot(xb, wcq_ref[...]), gq_ref[...]).astype(BF16)
    ckv = _rms_norm(_dot(xb, wckv_ref[...]), gkv_ref[...])
    ckv_ref[...] = ckv
    kpe = _dot(xb, wkpe_ref[...]) * cos + _dot(xb, wkpes_ref[...]) * sin
    kpe_ref[...] = kpe
    qa = _dot(cqn, wq_ref[...])
    qb = _dot(cqn, wqs_ref[...])
    cb = ckv.astype(BF16)
    kn = _dot(cb, wk_ref[...])
    for h in range(MLA_HEADS):
        sl = slice(h * HEAD_PAD, (h + 1) * HEAD_PAD)
        q_ref[:, sl] = ((qa[:, sl] * cos + qb[:, sl] * sin) * ATTN_SCALE).astype(BF16)
        k_ref[:, sl] = (kn[:, sl] + kpe).astype(BF16)
    v_ref[...] = _dot(cb, wv_ref[...]).astype(BF16)


def _mla_in(x, cos, sin, wcq, wckv, wkpe, wkpes, gq, gkv, wq, wqs, wk, wv, *, tm):
    n, d = x.shape
    hp = MLA_HEADS * HEAD_PAD
    row = lambda w: pl.BlockSpec((tm, w), lambda i: (i, 0))
    full = lambda a: pl.BlockSpec(a.shape, lambda i: (0,) * a.ndim)
    return pl.pallas_call(
        _mla_in_kernel,
        grid=(n // tm,),
        in_specs=[row(d), row(HEAD_PAD), row(HEAD_PAD), full(wcq), full(wckv), full(wkpe), full(wkpes),
                  full(gq), full(gkv), full(wq), full(wqs), full(wk), full(wv)],
        out_specs=[row(hp), row(hp), row(MLA_WIDTH), row(KV_LORA), row(HEAD_PAD)],
        out_shape=[jax.ShapeDtypeStruct((n, hp), BF16), jax.ShapeDtypeStruct((n, hp), BF16),
                   jax.ShapeDtypeStruct((n, MLA_WIDTH), BF16), jax.ShapeDtypeStruct((n, KV_LORA), F32),
                   jax.ShapeDtypeStruct((n, HEAD_PAD), F32)],
        compiler_params=_params("parallel"),
        name="mla_project",
    )(x, cos, sin, wcq, wckv, wkpe, wkpes, gq, gkv, wq, wqs, wk, wv)


def _flash_kernel(q_ref, k_ref, v_ref, o_ref, m_ref, l_ref, acc_ref, *, tq, tk, nk):
    qi = pl.program_id(1)
    ki = pl.program_id(2)

    @pl.when(ki == 0)
    def _():
        m_ref[...] = jnp.full(m_ref.shape, NEG_INF, F32)
        l_ref[...] = jnp.zeros(l_ref.shape, F32)
        acc_ref[...] = jnp.zeros(acc_ref.shape, F32)

    even_lane = lax.broadcasted_iota(jnp.int32, (tq, HEAD_PAD), 1) < V_HEAD

    @pl.when(ki * tk < (qi + 1) * tq)
    def _():
        q_chunk = (lax.broadcasted_iota(jnp.int32, (tq, tk), 0) + qi * tq) // CHUNK
        k_chunk = (lax.broadcasted_iota(jnp.int32, (tq, tk), 1) + ki * tk) // CHUNK
        mask = k_chunk <= q_chunk
        for p in range(MLA_HEADS // 2):
            vp = v_ref[0, :, p * HEAD_PAD:(p + 1) * HEAD_PAD]
            alphas, pvs = [], []
            for e in range(2):
                h = 2 * p + e
                sl = slice(h * HEAD_PAD, (h + 1) * HEAD_PAD)
                s = jnp.where(mask, _dot_nt(q_ref[0, :, sl], k_ref[0, :, sl]), NEG_INF)
                m_prev = m_ref[h]
                m_new = jnp.maximum(m_prev, jnp.max(s, axis=1, keepdims=True))
                alpha = jnp.exp(m_prev - m_new)
                pm = jnp.exp(s - m_new[:, 0:1])
                l_ref[h] = alpha * l_ref[h] + jnp.sum(pm, axis=1, keepdims=True)
                m_ref[h] = m_new
                alphas.append(alpha)
                pvs.append(_dot(pm.astype(BF16), vp))
            acc_ref[p] = (acc_ref[p] * jnp.where(even_lane, alphas[0], alphas[1])
                          + jnp.where(even_lane, pvs[0], pvs[1]))

    @pl.when(ki == nk - 1)
    def _():
        for p in range(MLA_HEADS // 2):
            l_pair = jnp.where(even_lane, l_ref[2 * p], l_ref[2 * p + 1])
            o_ref[0, :, p * HEAD_PAD:(p + 1) * HEAD_PAD] = (acc_ref[p] / l_pair).astype(o_ref.dtype)


def _flash(q, k, v, *, tq, tk):
    nb, s_len, hp = q.shape
    nq, nk = s_len // tq, s_len // tk
    last_k = lambda i: ((i + 1) * tq - 1) // tk
    return pl.pallas_call(
        functools.partial(_flash_kernel, tq=tq, tk=tk, nk=nk),
        grid=(nb, nq, nk),
        in_specs=[
            pl.BlockSpec((1, tq, hp), lambda b, i, j: (b, i, 0)),
            pl.BlockSpec((1, tk, hp), lambda b, i, j: (b, jnp.minimum(j, last_k(i)), 0)),
            pl.BlockSpec((1, tk, MLA_WIDTH), lambda b, i, j: (b, jnp.minimum(j, last_k(i)), 0)),
        ],
        out_specs=pl.BlockSpec((1, tq, MLA_WIDTH), lambda b, i, j: (b, i, 0)),
        out_shape=jax.ShapeDtypeStruct((nb, s_len, MLA_WIDTH), BF16),
        scratch_shapes=[pltpu.VMEM((MLA_HEADS, tq, HEAD_PAD), F32), pltpu.VMEM((MLA_HEADS, tq, HEAD_PAD), F32),
                        pltpu.VMEM((MLA_HEADS // 2, tq, HEAD_PAD), F32)],
        compiler_params=_params("parallel", "parallel", "arbitrary"),
        name="mla_attend_prompt",
    )(q, k, v)


def _attend_sample_kernel(q_ref, cc_ref, ck_ref, nc_ref, nk_ref, wuk_ref, wuv_ref, o_ref, q2_ref, ql_ref,
                          *, s, past, tk):
    for h in range(MLA_HEADS):
        qh = q_ref[0, :, h * HEAD_PAD:(h + 1) * HEAD_PAD]
        q2_ref[h * s:(h + 1) * s, :] = qh
        ql_ref[h * s:(h + 1) * s, :] = _dot(qh, wuk_ref[h]).astype(BF16)
    q2 = q2_ref[...]
    ql = ql_ref[...]
    rows = MLA_HEADS * s
    m = jnp.full((rows, 1), NEG_INF, F32)
    l = jnp.zeros((rows, 1), F32)
    acc = jnp.zeros((rows, KV_LORA), F32)
    tiles = [(cc_ref, ck_ref, j * tk, tk) for j in range(past // tk)] + [(nc_ref, nk_ref, 0, s)]
    for c_ref, kp_ref, start, size in tiles:
        c = c_ref[0, start:start + size, :].astype(BF16)
        kp = kp_ref[0, start:start + size, :].astype(BF16)
        sc = _dot_nt(ql, c) + _dot_nt(q2, kp)
        m_new = jnp.maximum(m, jnp.max(sc, axis=1, keepdims=True))
        alpha = jnp.exp(m - m_new)
        pm = jnp.exp(sc - m_new)
        l = alpha * l + jnp.sum(pm, axis=1, keepdims=True)
        acc = alpha * acc + _dot(pm.astype(BF16), c)
        m = m_new
    olat = (acc / l).astype(BF16)
    for p in range(MLA_HEADS // 2):
        he, ho = 2 * p, 2 * p + 1
        o_ref[0, :, p * HEAD_PAD:(p + 1) * HEAD_PAD] = (
            _dot(olat[he * s:(he + 1) * s, :], wuv_ref[he]) + _dot(olat[ho * s:(ho + 1) * s, :], wuv_ref[ho])
        ).astype(o_ref.dtype)


def _attend_sample(q, cache_ckv, cache_kpe_pad, new_ckv, new_kpe_pad, wuk_pad, wuv_pad, *, tk):
    nb, s, hp = q.shape
    past = cache_ckv.shape[1]
    return pl.pallas_call(
        functools.partial(_attend_sample_kernel, s=s, past=past, tk=tk),
        grid=(nb,),
        in_specs=[
            pl.BlockSpec((1, s, hp), lambda b: (b, 0, 0)),
            pl.BlockSpec((1, past, KV_LORA), lambda b: (b, 0, 0)),
            pl.BlockSpec((1, past, HEAD_PAD), lambda b: (b, 0, 0)),
            pl.BlockSpec((1, s, KV_LORA), lambda b: (b, 0, 0)),
            pl.BlockSpec((1, s, HEAD_PAD), lambda b: (b, 0, 0)),
            pl.BlockSpec(wuk_pad.shape, lambda b: (0, 0, 0)),
            pl.BlockSpec(wuv_pad.shape, lambda b: (0, 0, 0)),
        ],
        out_specs=pl.BlockSpec((1, s, MLA_WIDTH), lambda b: (b, 0, 0)),
        out_shape=jax.ShapeDtypeStruct((nb, s, MLA_WIDTH), BF16),
        scratch_shapes=[pltpu.VMEM((MLA_HEADS * s, HEAD_PAD), BF16), pltpu.VMEM((MLA_HEADS * s, KV_LORA), BF16)],
        compiler_params=_params("parallel"),
        name="mla_attend_sample",
    )(q, cache_ckv, cache_kpe_pad, new_ckv, new_kpe_pad, wuk_pad, wuv_pad)


def _sgu_kernel(x_ref, wu_ref, wv_ref, g_ref, b_ref, ws_ref, bs_ref, o_ref, vn_ref):
    xb = x_ref[...].astype(BF16)
    u = _dot(xb, wu_ref[...])
    vn = _layer_norm(_dot(xb, wv_ref[...]), g_ref[...], b_ref[...])
    vn_ref[...] = vn
    vb = vn.astype(BF16)
    for g in range(SG_GROUPS):
        sl = slice(g * SG_CH, (g + 1) * SG_CH)
        mixed = _dot(ws_ref[0, g], vb[:, sl]) + bs_ref[0, g]
        o_ref[:, sl] = (u[:, sl] * mixed).astype(o_ref.dtype)


def _sgu(x, wu, wv, g, b, ws, bs, *, tm, n_first):
    n, d = x.shape
    first_tiles = n_first // tm
    which = lambda i: (jnp.where(i < first_tiles, 0, 1), 0, 0, 0)
    full = lambda a: pl.BlockSpec(a.shape, lambda i: (0,) * a.ndim)
    return pl.pallas_call(
        _sgu_kernel,
        grid=(n // tm,),
        in_specs=[pl.BlockSpec((tm, d), lambda i: (i, 0)), full(wu), full(wv), full(g), full(b),
                  pl.BlockSpec((1, SG_GROUPS, tm, tm), which), pl.BlockSpec((1, SG_GROUPS, tm, SG_CH), which)],
        out_specs=[pl.BlockSpec((tm, SG_WIDTH), lambda i: (i, 0)), pl.BlockSpec((tm, SG_WIDTH), lambda i: (i, 0))],
        out_shape=[jax.ShapeDtypeStruct((n, SG_WIDTH), BF16), jax.ShapeDtypeStruct((n, SG_WIDTH), F32)],
        compiler_params=_params("parallel"),
        name="sgu_mixer",
    )(x, wu, wv, g, b, ws, bs)


def _sgu_token_mixers(w_s, b_s, chunk, tm):
    w = w_s[:, :chunk, :chunk] * jnp.tril(jnp.ones((chunk, chunk), w_s.dtype))
    rep = tm // chunk
    eye = jnp.eye(rep, dtype=w.dtype)
    wbd = (w[:, None, :, None, :] * eye[None, :, None, :, None]).reshape(SG_GROUPS, tm, tm)
    bias = jnp.broadcast_to(jnp.tile(b_s[:, :chunk], (1, rep))[:, :, None], (SG_GROUPS, tm, SG_CH))
    return wbd, bias


def _rope_tables(positions):
    half = QK_ROPE // 2
    inv = ROPE_THETA ** (-jnp.arange(half, dtype=F32) / half)
    ang = positions.astype(F32)[:, None] * inv[None, :]
    cos, sin = jnp.cos(ang), jnp.sin(ang)
    n = positions.shape[0]
    ones = jnp.ones((n, QK_NOPE), F32)
    zeros_n = jnp.zeros((n, QK_NOPE), F32)
    pad = jnp.zeros((n, HEAD_PAD - QK_NOPE - QK_ROPE), F32)
    cos_t = jnp.concatenate([ones, cos, cos, pad], axis=1)
    sin_t = jnp.concatenate([zeros_n, -sin, sin, pad], axis=1)
    return cos_t, sin_t


def _swap_halves(w):
    half = QK_ROPE // 2
    return jnp.concatenate([w[..., half:], w[..., :half]], axis=-1)


def _head_slots(nope, rope):
    r = (nope if nope is not None else rope).shape[0]
    parts = [nope if nope is not None else jnp.zeros((r, MLA_HEADS, QK_NOPE), F32),
             rope if rope is not None else jnp.zeros((r, MLA_HEADS, QK_ROPE), F32),
             jnp.zeros((r, MLA_HEADS, HEAD_PAD - QK_NOPE - QK_ROPE), F32)]
    return jnp.concatenate(parts, axis=-1).reshape(r, MLA_HEADS * HEAD_PAD)


def _rope_slot(w):
    r = w.shape[0]
    return jnp.concatenate([jnp.zeros((r, QK_NOPE), w.dtype), w,
                            jnp.zeros((r, HEAD_PAD - QK_NOPE - QK_ROPE), w.dtype)], axis=1)


TM_FFN = 512
TF_FFN = 512
TM_PROJ = 512
TM_OUT = 512
TM_MLA = 256
TM_SGU = 256
T_POOL = 512
T_SSM = 512
TQ_ATTN = 256
TK_ATTN = 512
TK_SAMPLE = 1024
SB_POOL = 8


def kernel(x_prompt, x_sample, cache_pool, state_ssm_re, state_ssm_im, cache_ckv, cache_kpe, ln_g, ln_b, ffn1_w1, ffn1_w3, ffn1_w2, ffn2_w1, ffn2_w3, ffn2_w2, w_in_e, pool_w, pool_scale, ssm_lam_re, ssm_lam_im, ssm_log_dt, ssm_b_re, ssm_b_im, ssm_c_re, ssm_c_im, ssm_d, ssm_w_glu, ssm_b_glu, w_out_e, w_in_o, mla_g_q, mla_g_kv, mla_w_uq, mla_w_uk, mla_w_uv, sg_g_v, sg_b_v, sg_w_s, sg_b_s, w_out_o):
    nbp, sp, d = x_prompt.shape
    nbs, ss, _ = x_sample.shape
    past = cache_ckv.shape[2]
    n_p, n_s = nbp * sp, nbs * ss
    x = jnp.concatenate([x_prompt.reshape(n_p, d), x_sample.reshape(n_s, d)], axis=0)
    bf = lambda a: a.astype(BF16)
    row = lambda a: a.reshape(1, -1)

    def ffn(x, w1, w3, w2, layer, k):
        return _ffn(x, bf(w1[layer]), bf(w3[layer]), bf(w2[layer]), row(ln_g[layer, k]), row(ln_b[layer, k]),
                    tm=min(TM_FFN, n_s), tf=TF_FFN)

    x = ffn(x, ffn1_w1, ffn1_w3, ffn1_w2, 0, 0)
    z = _proj(x, bf(w_in_e[0]), tm=min(TM_PROJ, n_s))
    z_p = z[:n_p].reshape(nbp, sp, d)
    z_s = z[n_p:].reshape(nbs, ss, d)
    pw = bf(pool_w[0])
    psc = pool_scale[0].reshape(len(POOL_WINDOWS), 1, POOL_CH)
    a_p = _pool(z_p, z_p, pw, psc, sb=1, t=min(T_POOL, sp), pos0=0, zero_first=True)
    hist = jnp.pad(cache_pool[0], ((0, 0), (POOL_HALO - POOL_HIST, 0), (0, 0)))
    sbp = math.gcd(SB_POOL, nbs)
    a_s = _pool(z_s, hist, pw, psc, sb=sbp, t=ss, pos0=past, zero_first=False)

    ab_re, ab_im, bbt_re, bbt_im = _ssm_prep(ssm_lam_re[0], ssm_lam_im[0], ssm_log_dt[0], ssm_b_re[0], ssm_b_im[0])
    bd_b_re = bf(_block_diag(bbt_re))
    bd_b_im = bf(_block_diag(bbt_im))
    bd_c_re = bf(_block_diag(ssm_c_re[0].transpose(0, 2, 1)))
    bd_c_im = bf(_block_diag(ssm_c_im[0].transpose(0, 2, 1)))
    a_re = ab_re.reshape(1, SSM_LANES)
    a_im = ab_im.reshape(1, SSM_LANES)
    ssm_args = (bd_b_re, bd_b_im, a_re, a_im, bd_c_re, bd_c_im, row(ssm_d[0]), bf(ssm_w_glu[0]), row(ssm_b_glu[0]))
    zero_state = jnp.zeros((nbp, 1, SSM_LANES), F32)
    b_p, sre_p, sim_p = _ssm(z_p, zero_state, zero_state, *ssm_args, t=min(T_SSM, sp))
    b_s, sre_s, sim_s = _ssm(z_s, state_ssm_re[0].reshape(nbs, 1, SSM_LANES),
                             state_ssm_im[0].reshape(nbs, 1, SSM_LANES), *ssm_args, t=ss)
    a_all = jnp.concatenate([a_p.reshape(n_p, POOL_WIDTH), a_s.reshape(n_s, POOL_WIDTH)], axis=0)
    b_all = jnp.concatenate([b_p.reshape(n_p, SSM_WIDTH), b_s.reshape(n_s, SSM_WIDTH)], axis=0)
    woe = bf(w_out_e[0])
    x = _out_ln(x, a_all, b_all, woe[:POOL_WIDTH], woe[POOL_WIDTH:], row(ln_g[0, 1]), row(ln_b[0, 1]),
                tm=min(TM_OUT, n_s))
    x = ffn(x, ffn2_w1, ffn2_w3, ffn2_w2, 0, 2)

    pool_p = z_p[:, sp - POOL_HIST:, :POOL_WIDTH][None]
    pool_s = z_s[:, ss - POOL_HIST:, :POOL_WIDTH][None]
    state_shape = lambda nb: (1, nb, SSM_GROUPS, SSM_STATE)

    x = ffn(x, ffn1_w1, ffn1_w3, ffn1_w2, 1, 0)
    wio = w_in_o[0]
    w_kpe = wio[:, Q_LORA + KV_LORA:MLA_IN]
    wuq = mla_w_uq[0]
    wq = _head_slots(wuq[:, :, :QK_NOPE], wuq[:, :, QK_NOPE:])
    wqs = _head_slots(None, _swap_halves(wuq[:, :, QK_NOPE:]))
    wk = _head_slots(mla_w_uk[0], None)
    wv = mla_w_uv[0].reshape(KV_LORA, MLA_WIDTH)
    positions = jnp.concatenate([jnp.tile(jnp.arange(sp, dtype=jnp.int32), nbp),
                                 jnp.tile(past + jnp.arange(ss, dtype=jnp.int32), nbs)])
    cos_t, sin_t = _rope_tables(positions)
    q_all, k_all, v_all, ckv_all, kpe_all = _mla_in(
        x, cos_t, sin_t, bf(wio[:, :Q_LORA]), bf(wio[:, Q_LORA:Q_LORA + KV_LORA]), bf(_rope_slot(w_kpe)),
        bf(_rope_slot(_swap_halves(w_kpe))), row(mla_g_q[0]), row(mla_g_kv[0]), bf(wq), bf(wqs), bf(wk), bf(wv),
        tm=min(TM_MLA, n_s))
    hp = MLA_HEADS * HEAD_PAD
    att_p = _flash(q_all[:n_p].reshape(nbp, sp, hp), k_all[:n_p].reshape(nbp, sp, hp),
                   v_all[:n_p].reshape(nbp, sp, MLA_WIDTH), tq=min(TQ_ATTN, sp), tk=min(TK_ATTN, sp))
    ckv_s = ckv_all[n_p:].reshape(nbs, ss, KV_LORA)
    kpe_s = kpe_all[n_p:].reshape(nbs, ss, HEAD_PAD)
    cache_kpe_pad = bf(jnp.pad(cache_kpe[0], ((0, 0), (0, 0), (QK_NOPE, HEAD_PAD - QK_NOPE - QK_ROPE))))
    wuk_pad = bf(jnp.pad(mla_w_uk[0].transpose(1, 2, 0), ((0, 0), (0, HEAD_PAD - QK_NOPE), (0, 0))))
    wuv_h = mla_w_uv[0].transpose(1, 0, 2)
    odd = (jnp.arange(MLA_HEADS) % 2 == 1)[:, None, None]
    wuv_pad = bf(jnp.where(odd, jnp.pad(wuv_h, ((0, 0), (0, 0), (V_HEAD, 0))),
                           jnp.pad(wuv_h, ((0, 0), (0, 0), (0, V_HEAD)))))
    att_s = _attend_sample(q_all[n_p:].reshape(nbs, ss, hp), cache_ckv[0], cache_kpe_pad, ckv_s, kpe_s,
                           wuk_pad, wuv_pad, tk=min(TK_SAMPLE, past))
    att = jnp.concatenate([att_p.reshape(n_p, MLA_WIDTH), att_s.reshape(n_s, MLA_WIDTH)], axis=0)

    tm_sgu = min(TM_SGU, n_s)
    ws_p, bs_p = _sgu_token_mixers(sg_w_s[0], sg_b_s[0], min(sp, SG_CHUNK), tm_sgu)
    ws_s, bs_s = _sgu_token_mixers(sg_w_s[0], sg_b_s[0], min(ss, SG_CHUNK), tm_sgu)
    sg_out, sgv_all = _sgu(x, bf(wio[:, MLA_IN:MLA_IN + SG_WIDTH]), bf(wio[:, MLA_IN + SG_WIDTH:]),
                           row(sg_g_v[0]), row(sg_b_v[0]), bf(jnp.stack([ws_p, ws_s])),
                           jnp.stack([bs_p, bs_s]), tm=tm_sgu, n_first=n_p)
    woo = bf(w_out_o[0])
    x = _out_ln(x, att, sg_out, woo[:MLA_WIDTH], woo[MLA_WIDTH:], row(ln_g[1, 1]), row(ln_b[1, 1]),
                tm=min(TM_OUT, n_s))
    x = ffn(x, ffn2_w1, ffn2_w3, ffn2_w2, 1, 2)

    rope_lanes = slice(QK_NOPE, QK_NOPE + QK_ROPE)
    return (x[:n_p].reshape(nbp, sp, d), x[n_p:].reshape(nbs, ss, d),
            pool_p, sre_p.reshape(state_shape(nbp)), sim_p.reshape(state_shape(nbp)),
            ckv_all[:n_p].reshape(1, nbp, sp, KV_LORA), kpe_all[:n_p, rope_lanes].reshape(1, nbp, sp, QK_ROPE),
            pool_s, sre_s.reshape(state_shape(nbs)), sim_s.reshape(state_shape(nbs)),
            ckv_s[None], kpe_s[:, :, rope_lanes][None],
            sgv_all[n_p:].reshape(1, nbs, ss, SG_WIDTH))
```

```python
import functools
import math

import jax
import jax.numpy as jnp
from jax import lax
from jax.experimental import pallas as pl
from jax.experimental.pallas import tpu as pltpu

F32 = jnp.float32
BF16 = jnp.bfloat16

D_MODEL = 2048
DEPTH = 2
CHUNK = 64
D_FF = 5632
ALPHA = (2 * DEPTH) ** 0.25
LN_EPS = 1e-5
RMS_EPS = 1e-6
POOL_WINDOWS = (2, 4, 8, 16)
POOL_CH = 384
POOL_WIDTH = len(POOL_WINDOWS) * POOL_CH
POOL_HIST = max(POOL_WINDOWS) - 1
POOL_HALO = 16
SSM_WIDTH = D_MODEL - POOL_WIDTH
SSM_GROUP_CH = 16
SSM_GROUPS = SSM_WIDTH // SSM_GROUP_CH
SSM_STATE = 64
SSM_LANES = SSM_GROUPS * SSM_STATE
MLA_HEADS = 16
Q_LORA = 512
KV_LORA = 256
QK_NOPE = 64
QK_ROPE = 32
V_HEAD = 64
HEAD_PAD = 128
MLA_WIDTH = MLA_HEADS * V_HEAD
MLA_IN = Q_LORA + KV_LORA + QK_ROPE
ATTN_SCALE = (QK_NOPE + QK_ROPE) ** -0.5
Q_SCALE = ATTN_SCALE * math.log2(math.e)
ROPE_THETA = 10000.0
SG_CHUNK = 128
SG_GROUPS = 8
SG_WIDTH = D_MODEL - MLA_WIDTH
SG_CH = SG_WIDTH // SG_GROUPS
NEG_INF = -1e30

VMEM_LIMIT_BYTES = 56 * 1024 * 1024


def _params(*semantics):
    return pltpu.CompilerParams(dimension_semantics=semantics, vmem_limit_bytes=VMEM_LIMIT_BYTES)


def _dot(a, b):
    return jnp.dot(a, b, preferred_element_type=F32)


def _dot_nt(a, b):
    return lax.dot_general(a, b, (((1,), (1,)), ((), ())), preferred_element_type=F32)


def _layer_norm(y, g, b):
    mu = jnp.mean(y, axis=-1, keepdims=True)
    d = y - mu
    var = jnp.mean(d * d, axis=-1, keepdims=True)
    return d * lax.rsqrt(var + LN_EPS) * g + b


def _rms_norm(y, g):
    return y * lax.rsqrt(jnp.mean(y * y, axis=-1, keepdims=True) + RMS_EPS) * g


def _sigmoid(y):
    return 1.0 / (1.0 + jnp.exp(-y))


def _gelu_tanh(y):
    c = math.sqrt(2.0 / math.pi)
    return 0.5 * y * (1.0 + jnp.tanh(c * (y + 0.044715 * (y * y * y))))


def _ffn_kernel(x_ref, w1_ref, w3_ref, w2_ref, g_ref, b_ref, o_ref, xb_ref, *, nf):
    f = pl.program_id(1)

    @pl.when(f == 0)
    def _():
        xb_ref[...] = x_ref[...].astype(BF16)

    xb = xb_ref[...]
    h1 = _dot(xb, w1_ref[...])
    h3 = _dot(xb, w3_ref[...])
    h = (h1 * _sigmoid(h1) * h3).astype(BF16)
    c = _dot(h, w2_ref[...])

    @pl.when(f == 0)
    def _():
        o_ref[...] = c

    @pl.when(f > 0)
    def _():
        o_ref[...] += c

    @pl.when(f == nf - 1)
    def _():
        y = ALPHA * x_ref[...] + 0.5 * o_ref[...]
        o_ref[...] = _layer_norm(y, g_ref[...], b_ref[...])


def _ffn(x, w1, w3, w2, g, b, *, tm, tf):
    n, d = x.shape
    nf = w1.shape[1] // tf
    return pl.pallas_call(
        functools.partial(_ffn_kernel, nf=nf),
        grid=(n // tm, nf),
        in_specs=[
            pl.BlockSpec((tm, d), lambda i, j: (i, 0)),
            pl.BlockSpec((d, tf), lambda i, j: (0, j)),
            pl.BlockSpec((d, tf), lambda i, j: (0, j)),
            pl.BlockSpec((tf, d), lambda i, j: (j, 0)),
            pl.BlockSpec((1, d), lambda i, j: (0, 0)),
            pl.BlockSpec((1, d), lambda i, j: (0, 0)),
        ],
        out_specs=pl.BlockSpec((tm, d), lambda i, j: (i, 0)),
        out_shape=jax.ShapeDtypeStruct((n, d), F32),
        scratch_shapes=[pltpu.VMEM((tm, d), BF16)],
        compiler_params=_params("parallel", "arbitrary"),
        name="ffn_ln",
    )(x, w1, w3, w2, g, b)


def _proj_kernel(x_ref, w_ref, o_ref):
    o_ref[...] = _dot(x_ref[...].astype(BF16), w_ref[...])


def _proj(x, w, *, tm):
    n, d = x.shape
    m = w.shape[1]
    return pl.pallas_call(
        _proj_kernel,
        grid=(n // tm,),
        in_specs=[pl.BlockSpec((tm, d), lambda i: (i, 0)), pl.BlockSpec((d, m), lambda i: (0, 0))],
        out_specs=pl.BlockSpec((tm, m), lambda i: (i, 0)),
        out_shape=jax.ShapeDtypeStruct((n, m), F32),
        compiler_params=_params("parallel"),
        name="proj_in",
    )(x, w)


def _out_ln_kernel(x_ref, a_ref, b_ref, wa_ref, wb_ref, g_ref, beta_ref, o_ref):
    mix = _dot(a_ref[...], wa_ref[...]) + _dot(b_ref[...], wb_ref[...])
    o_ref[...] = _layer_norm(ALPHA * x_ref[...] + mix, g_ref[...], beta_ref[...])


def _out_ln(x, a, b, wa, wb, g, beta, *, tm):
    n, d = x.shape
    ka, kb = a.shape[1], b.shape[1]
    return pl.pallas_call(
        _out_ln_kernel,
        grid=(n // tm,),
        in_specs=[
            pl.BlockSpec((tm, d), lambda i: (i, 0)),
            pl.BlockSpec((tm, ka), lambda i: (i, 0)),
            pl.BlockSpec((tm, kb), lambda i: (i, 0)),
            pl.BlockSpec((ka, d), lambda i: (0, 0)),
            pl.BlockSpec((kb, d), lambda i: (0, 0)),
            pl.BlockSpec((1, d), lambda i: (0, 0)),
            pl.BlockSpec((1, d), lambda i: (0, 0)),
        ],
        out_specs=pl.BlockSpec((tm, d), lambda i: (i, 0)),
        out_shape=jax.ShapeDtypeStruct((n, d), F32),
        compiler_params=_params("parallel"),
        name="out_ln",
    )(x, a, b, wa, wb, g, beta)


def _pool_kernel(u_ref, halo_ref, w_ref, sc_ref, o_ref, buf_ref, *, sb, t, pos0, zero_first):
    ti = pl.program_id(1)
    pos = lax.broadcasted_iota(jnp.int32, (t, 1), 0) + (ti * t + pos0 + 1)
    for s in range(sb):
        halo = halo_ref[s]
        if zero_first:
            halo = jnp.where(ti == 0, 0.0, halo)
        buf_ref[0:POOL_HALO, :] = halo
        buf_ref[POOL_HALO:POOL_HALO + t, :] = u_ref[s]
        for g, w in enumerate(POOL_WINDOWS):
            sl = slice(g * POOL_CH, (g + 1) * POOL_CH)
            u = buf_ref[POOL_HALO:POOL_HALO + t, sl]
            acc = u
            for k in range(1, w):
                acc = acc + buf_ref[POOL_HALO - k:POOL_HALO - k + t, sl]
            cnt = jnp.minimum(pos, w).astype(F32)
            d = (acc / cnt - u).astype(BF16)
            o_ref[s, :, sl] = (_dot(d, w_ref[g]) * sc_ref[g]).astype(o_ref.dtype)


def _pool(z3, halo_src, w, sc, *, sb, t, pos0, zero_first):
    n_seq, s_len, _ = z3.shape
    if zero_first:
        hb = t // POOL_HALO
        halo_map = lambda i, j: (i, jnp.maximum(j * hb - 1, 0), 0)
    else:
        halo_map = lambda i, j: (i, 0, 0)
    return pl.pallas_call(
        functools.partial(_pool_kernel, sb=sb, t=t, pos0=pos0, zero_first=zero_first),
        grid=(n_seq // sb, s_len // t),
        in_specs=[
            pl.BlockSpec((sb, t, POOL_WIDTH), lambda i, j: (i, j, 0)),
            pl.BlockSpec((sb, POOL_HALO, POOL_WIDTH), halo_map),
            pl.BlockSpec((len(POOL_WINDOWS), POOL_CH, POOL_CH), lambda i, j: (0, 0, 0)),
            pl.BlockSpec((len(POOL_WINDOWS), 1, POOL_CH), lambda i, j: (0, 0, 0)),
        ],
        out_specs=pl.BlockSpec((sb, t, POOL_WIDTH), lambda i, j: (i, j, 0)),
        out_shape=jax.ShapeDtypeStruct((n_seq, s_len, POOL_WIDTH), BF16),
        scratch_shapes=[pltpu.VMEM((POOL_HALO + t, POOL_WIDTH), F32)],
        compiler_params=_params("parallel", "arbitrary"),
        name="pool_mixer",
    )(z3, halo_src, w, sc)


def _ssm_prep_kernel(lr_ref, li_ref, ldt_ref, br_ref, bi_ref, abr_ref, abi_ref, bbr_ref, bbi_ref):
    lr = lr_ref[...]
    li = li_ref[...]
    dt = jnp.exp(ldt_ref[...])
    mag = jnp.exp(lr * dt)
    abr = mag * jnp.cos(li * dt)
    abi = mag * jnp.sin(li * dt)
    den = lr * lr + li * li
    nr = abr - 1.0
    cor = (nr * lr + abi * li) / den
    coi = (abi * lr - nr * li) / den
    br = br_ref[...]
    bi = bi_ref[...]
    abr_ref[...] = abr
    abi_ref[...] = abi
    bbr_ref[...] = cor * br - coi * bi
    bbi_ref[...] = cor * bi + coi * br


def _ssm_prep(lam_re, lam_im, log_dt, b_re, b_im):
    g, p = lam_re.shape
    c = b_re.shape[-1]
    lr = lam_re.reshape(g, 1, p)
    li = lam_im.reshape(g, 1, p)
    ldt = jnp.broadcast_to(log_dt.reshape(g, 1, 1), (g, 1, p))
    brt = b_re.transpose(0, 2, 1)
    bit = b_im.transpose(0, 2, 1)
    return pl.pallas_call(
        _ssm_prep_kernel,
        out_shape=[jax.ShapeDtypeStruct((g, 1, p), F32), jax.ShapeDtypeStruct((g, 1, p), F32),
                   jax.ShapeDtypeStruct((g, c, p), F32), jax.ShapeDtypeStruct((g, c, p), F32)],
        name="ssm_discretise",
    )(lr, li, ldt, brt, bit)


SSM_LANE_SPLIT = 2


def _ssm_kernel(u_ref, hre_ref, him_ref, bbr_ref, bbi_ref, ar_ref, ai_ref, cr_ref, ci_ref, d_ref,
                wg_ref, bg_ref, o_ref, sre_ref, sim_ref, xr_ref, xi_ref, st_ref, *, t, nt):
    ti = pl.program_id(1)

    @pl.when(ti == 0)
    def _():
        st_ref[0:1, :] = hre_ref[0]
        st_ref[1:2, :] = him_ref[0]

    u = u_ref[0]
    ub = u.astype(BF16)
    xr_ref[...] = _dot(ub, bbr_ref[...])
    xi_ref[...] = _dot(ub, bbi_ref[...])

    slab = SSM_LANES // SSM_LANE_SPLIT
    for c in range(SSM_LANE_SPLIT):
        sl = slice(c * slab, (c + 1) * slab)
        ar = ar_ref[:, sl]
        ai = ai_ref[:, sl]

        def step(i, carry, sl=sl, ar=ar, ai=ai):
            pr, pi = carry
            nr = ar * pr - ai * pi + xr_ref[pl.ds(i, 1), sl]
            ni = ar * pi + ai * pr + xi_ref[pl.ds(i, 1), sl]
            xr_ref[pl.ds(i, 1), sl] = nr
            xi_ref[pl.ds(i, 1), sl] = ni
            return nr, ni

        pr, pi = lax.fori_loop(0, t, step, (st_ref[0:1, sl], st_ref[1:2, sl]))
        st_ref[0:1, sl] = pr
        st_ref[1:2, sl] = pi

    y = (_dot(xr_ref[...].astype(BF16), cr_ref[...]) - _dot(xi_ref[...].astype(BF16), ci_ref[...])
         + d_ref[...] * u)
    gl = _gelu_tanh(y)
    o_ref[0] = (gl * _sigmoid(_dot(gl.astype(BF16), wg_ref[...]) + bg_ref[...])).astype(o_ref.dtype)

    @pl.when(ti == nt - 1)
    def _():
        sre_ref[0] = st_ref[0:1, :]
        sim_ref[0] = st_ref[1:2, :]


def _ssm(z3, h_re, h_im, bd_b_re, bd_b_im, a_re, a_im, bd_c_re, bd_c_im, d_skip, w_glu, b_glu, *, t):
    n_seq, s_len, _ = z3.shape
    nt = s_len // t
    col = POOL_WIDTH // SSM_WIDTH
    const2 = lambda i, j: (0, 0)
    state_spec = pl.BlockSpec((1, 1, SSM_LANES), lambda i, j: (i, 0, 0))
    return pl.pallas_call(
        functools.partial(_ssm_kernel, t=t, nt=nt),
        grid=(n_seq, nt),
        in_specs=[
            pl.BlockSpec((1, t, SSM_WIDTH), lambda i, j: (i, j, col)),
            state_spec, state_spec,
            pl.BlockSpec((SSM_WIDTH, SSM_LANES), const2),
            pl.BlockSpec((SSM_WIDTH, SSM_LANES), const2),
            pl.BlockSpec((1, SSM_LANES), const2),
            pl.BlockSpec((1, SSM_LANES), const2),
            pl.BlockSpec((SSM_LANES, SSM_WIDTH), const2),
            pl.BlockSpec((SSM_LANES, SSM_WIDTH), const2),
            pl.BlockSpec((1, SSM_WIDTH), const2),
            pl.BlockSpec((SSM_WIDTH, SSM_WIDTH), const2),
            pl.BlockSpec((1, SSM_WIDTH), const2),
        ],
        out_specs=[pl.BlockSpec((1, t, SSM_WIDTH), lambda i, j: (i, j, 0)), state_spec, state_spec],
        out_shape=[jax.ShapeDtypeStruct((n_seq, s_len, SSM_WIDTH), BF16),
                   jax.ShapeDtypeStruct((n_seq, 1, SSM_LANES), F32),
                   jax.ShapeDtypeStruct((n_seq, 1, SSM_LANES), F32)],
        scratch_shapes=[pltpu.VMEM((t, SSM_LANES), F32), pltpu.VMEM((t, SSM_LANES), F32),
                        pltpu.VMEM((8, SSM_LANES), F32)],
        compiler_params=_params("parallel", "arbitrary"),
        name="s5_mixer",
    )(z3, h_re, h_im, bd_b_re, bd_b_im, a_re, a_im, bd_c_re, bd_c_im, d_skip, w_glu, b_glu)


def _block_diag(blocks):
    g, r, c = blocks.shape
    eye = jnp.eye(g, dtype=blocks.dtype)
    return (blocks[:, :, None, :] * eye[:, None, :, None]).reshape(g * r, g * c)


def _mla_in_kernel(x_ref, cos_ref, sin_ref, wcq_ref, wckv_ref, wkpe_ref, wkpes_ref, gq_ref, gkv_ref,
                   wq_ref, wqs_ref, wk_ref, wv_ref, q_ref, k_ref, v_ref, ckv_ref, kpe_ref):
    xb = x_ref[...].astype(BF16)
    cos = cos_ref[...]
    sin = sin_ref[...]
    cqn = _rms_norm(_dot(xb, wcq_ref[...]), gq_ref[...]).astype(BF16)
    ckv = _rms_norm(_dot(xb, wckv_ref[...]), gkv_ref[...])
    ckv_ref[...] = ckv
    kpe = _dot(xb, wkpe_ref[...]) * cos + _dot(xb, wkpes_ref[...]) * sin
    kpe_ref[...] = kpe
    qa = _dot(cqn, wq_ref[...])
    qb = _dot(cqn, wqs_ref[...])
    cb = ckv.astype(BF16)
    kn = _dot(cb, wk_ref[...])
    for h in range(MLA_HEADS):
        sl = slice(h * HEAD_PAD, (h + 1) * HEAD_PAD)
        q_ref[:, sl] = ((qa[:, sl] * cos + qb[:, sl] * sin) * Q_SCALE).astype(BF16)
        k_ref[:, sl] = (kn[:, sl] + kpe).astype(BF16)
    v_ref[...] = _dot(cb, wv_ref[...]).astype(BF16)


def _mla_in(x, cos, sin, wcq, wckv, wkpe, wkpes, gq, gkv, wq, wqs, wk, wv, *, tm):
    n, d = x.shape
    hp = MLA_HEADS * HEAD_PAD
    row = lambda w: pl.BlockSpec((tm, w), lambda i: (i, 0))
    full = lambda a: pl.BlockSpec(a.shape, lambda i: (0,) * a.ndim)
    return pl.pallas_call(
        _mla_in_kernel,
        grid=(n // tm,),
        in_specs=[row(d), row(HEAD_PAD), row(HEAD_PAD), full(wcq), full(wckv), full(wkpe), full(wkpes),
                  full(gq), full(gkv), full(wq), full(wqs), full(wk), full(wv)],
        out_specs=[row(hp), row(hp), row(MLA_WIDTH), row(KV_LORA), row(HEAD_PAD)],
        out_shape=[jax.ShapeDtypeStruct((n, hp), BF16), jax.ShapeDtypeStruct((n, hp), BF16),
                   jax.ShapeDtypeStruct((n, MLA_WIDTH), BF16), jax.ShapeDtypeStruct((n, KV_LORA), F32),
                   jax.ShapeDtypeStruct((n, HEAD_PAD), F32)],
        compiler_params=_params("parallel"),
        name="mla_project",
    )(x, cos, sin, wcq, wckv, wkpe, wkpes, gq, gkv, wq, wqs, wk, wv)


def _flash_kernel(qt_ref, kt_ref, q_ref, k_ref, v_ref, o_ref, m_ref, l_ref, acc_ref, *, tq, tk):
    t = pl.program_id(1)
    qi = qt_ref[t]
    ki = kt_ref[t]
    q0 = qi * tq
    k0 = ki * tk

    @pl.when(ki == 0)
    def _():
        m_ref[...] = jnp.full(m_ref.shape, NEG_INF, F32)
        l_ref[...] = jnp.zeros(l_ref.shape, F32)
        acc_ref[...] = jnp.zeros(acc_ref.shape, F32)

    def tile(masked):
        if masked:
            k_chunk = (lax.broadcasted_iota(jnp.int32, (tk, tq), 0) + k0) // CHUNK
            q_chunk = (lax.broadcasted_iota(jnp.int32, (tk, tq), 1) + q0) // CHUNK
            mask = k_chunk <= q_chunk
        def scores(h):
            return _dot(k_ref[0, :, h * HEAD_PAD:(h + 1) * HEAD_PAD], q_ref[0, h * HEAD_PAD:(h + 1) * HEAD_PAD, :])

        s_next = scores(0)
        for h in range(MLA_HEADS):
            s = s_next
            if h + 1 < MLA_HEADS:
                s_next = scores(h + 1)
            if masked:
                s = jnp.where(mask, s, NEG_INF)
            s3 = s.reshape(tk // 8, 8, tq)
            m_prev = m_ref[h]
            m_new = jnp.maximum(m_prev, jnp.max(jnp.max(s3, axis=0), axis=0, keepdims=True))
            alpha = jnp.exp2(m_prev - m_new)
            p3 = jnp.exp2(s3 - m_new[None])
            l_ref[h] = alpha * l_ref[h] + jnp.sum(p3, axis=0)
            m_ref[h] = m_new
            pv = _dot(v_ref[0, h * V_HEAD:(h + 1) * V_HEAD, :], p3.reshape(tk, tq).astype(BF16))
            rows = slice(h * V_HEAD, (h + 1) * V_HEAD)
            acc = acc_ref[rows, :].reshape(V_HEAD // 8, 8, tq) * alpha[None] + pv.reshape(V_HEAD // 8, 8, tq)
            acc_ref[rows, :] = acc.reshape(V_HEAD, tq)

    interior = k0 + tk <= q0 + CHUNK

    @pl.when(interior)
    def _():
        tile(False)

    @pl.when(jnp.logical_not(interior))
    def _():
        tile(True)

    @pl.when(k0 + tk >= q0 + tq)
    def _():
        for h in range(MLA_HEADS):
            rows = slice(h * V_HEAD, (h + 1) * V_HEAD)
            l_row = jnp.sum(l_ref[h], axis=0, keepdims=True)
            o_ref[0, rows, :] = (acc_ref[rows, :] / l_row).astype(o_ref.dtype)


def _flash(q_t, k, v_t, *, tq, tk):
    nb, hp, s_len = q_t.shape
    pairs = [(i, j) for i in range(s_len // tq) for j in range(s_len // tk) if j * tk < (i + 1) * tq]
    q_tab = jnp.asarray([p[0] for p in pairs], jnp.int32)
    k_tab = jnp.asarray([p[1] for p in pairs], jnp.int32)
    grid_spec = pltpu.PrefetchScalarGridSpec(
        num_scalar_prefetch=2,
        grid=(nb, len(pairs)),
        in_specs=[
            pl.BlockSpec((1, hp, tq), lambda b, t, qt, kt: (b, 0, qt[t])),
            pl.BlockSpec((1, tk, hp), lambda b, t, qt, kt: (b, kt[t], 0)),
            pl.BlockSpec((1, MLA_WIDTH, tk), lambda b, t, qt, kt: (b, 0, kt[t])),
        ],
        out_specs=pl.BlockSpec((1, MLA_WIDTH, tq), lambda b, t, qt, kt: (b, 0, qt[t])),
        scratch_shapes=[pltpu.VMEM((MLA_HEADS, 8, tq), F32), pltpu.VMEM((MLA_HEADS, 8, tq), F32),
                        pltpu.VMEM((MLA_WIDTH, tq), F32)],
    )
    return pl.pallas_call(
        functools.partial(_flash_kernel, tq=tq, tk=tk),
        grid_spec=grid_spec,
        out_shape=jax.ShapeDtypeStruct((nb, MLA_WIDTH, s_len), BF16),
        compiler_params=_params("parallel", "arbitrary"),
        name="mla_attend_prompt",
    )(q_tab, k_tab, q_t, k, v_t)


def _attend_sample_kernel(q_ref, cc_ref, ck_ref, nc_ref, nk_ref, wuk_ref, wuv_ref, o_ref, q2_ref, ql_ref,
                          *, s, past, tk):
    for h in range(MLA_HEADS):
        qh = q_ref[0, :, h * HEAD_PAD:(h + 1) * HEAD_PAD]
        q2_ref[h * s:(h + 1) * s, :] = qh
        ql_ref[h * s:(h + 1) * s, :] = _dot(qh, wuk_ref[h]).astype(BF16)
    q2 = q2_ref[...]
    ql = ql_ref[...]
    rows = MLA_HEADS * s
    m = jnp.full((rows, 1), NEG_INF, F32)
    l = jnp.zeros((rows, 1), F32)
    acc = jnp.zeros((rows, KV_LORA), F32)
    tiles = [(cc_ref, ck_ref, j * tk, tk) for j in range(past // tk)] + [(nc_ref, nk_ref, 0, s)]
    for c_ref, kp_ref, start, size in tiles:
        c = c_ref[0, start:start + size, :].astype(BF16)
        kp = kp_ref[0, start:start + size, :].astype(BF16)
        sc = _dot_nt(ql, c) + _dot_nt(q2, kp)
        m_new = jnp.maximum(m, jnp.max(sc, axis=1, keepdims=True))
        alpha = jnp.exp2(m - m_new)
        pm = jnp.exp2(sc - m_new)
        l = alpha * l + jnp.sum(pm, axis=1, keepdims=True)
        acc = alpha * acc + _dot(pm.astype(BF16), c)
        m = m_new
    olat = (acc / l).astype(BF16)
    for p in range(MLA_HEADS // 2):
        he, ho = 2 * p, 2 * p + 1
        o_ref[0, :, p * HEAD_PAD:(p + 1) * HEAD_PAD] = (
            _dot(olat[he * s:(he + 1) * s, :], wuv_ref[he]) + _dot(olat[ho * s:(ho + 1) * s, :], wuv_ref[ho])
        ).astype(o_ref.dtype)


def _attend_sample(q, cache_ckv, cache_kpe_pad, new_ckv, new_kpe_pad, wuk_pad, wuv_pad, *, tk):
    nb, s, hp = q.shape
    past = cache_ckv.shape[1]
    return pl.pallas_call(
        functools.partial(_attend_sample_kernel, s=s, past=past, tk=tk),
        grid=(nb,),
        in_specs=[
            pl.BlockSpec((1, s, hp), lambda b: (b, 0, 0)),
            pl.BlockSpec((1, past, KV_LORA), lambda b: (b, 0, 0)),
            pl.BlockSpec((1, past, HEAD_PAD), lambda b: (b, 0, 0)),
            pl.BlockSpec((1, s, KV_LORA), lambda b: (b, 0, 0)),
            pl.BlockSpec((1, s, HEAD_PAD), lambda b: (b, 0, 0)),
            pl.BlockSpec(wuk_pad.shape, lambda b: (0, 0, 0)),
            pl.BlockSpec(wuv_pad.shape, lambda b: (0, 0, 0)),
        ],
        out_specs=pl.BlockSpec((1, s, MLA_WIDTH), lambda b: (b, 0, 0)),
        out_shape=jax.ShapeDtypeStruct((nb, s, MLA_WIDTH), BF16),
        scratch_shapes=[pltpu.VMEM((MLA_HEADS * s, HEAD_PAD), BF16), pltpu.VMEM((MLA_HEADS * s, KV_LORA), BF16)],
        compiler_params=_params("parallel"),
        name="mla_attend_sample",
    )(q, cache_ckv, cache_kpe_pad, new_ckv, new_kpe_pad, wuk_pad, wuv_pad)


def _sgu_kernel(x_ref, wu_ref, wv_ref, g_ref, b_ref, ws_ref, bs_ref, o_ref, vn_ref):
    xb = x_ref[...].astype(BF16)
    u = _dot(xb, wu_ref[...])
    vn = _layer_norm(_dot(xb, wv_ref[...]), g_ref[...], b_ref[...])
    vn_ref[...] = vn
    vb = vn.astype(BF16)
    for g in range(SG_GROUPS):
        sl = slice(g * SG_CH, (g + 1) * SG_CH)
        mixed = _dot(ws_ref[0, g], vb[:, sl]) + bs_ref[0, g]
        o_ref[:, sl] = (u[:, sl] * mixed).astype(o_ref.dtype)


def _sgu(x, wu, wv, g, b, ws, bs, *, tm, n_first):
    n, d = x.shape
    first_tiles = n_first // tm
    which = lambda i: (jnp.where(i < first_tiles, 0, 1), 0, 0, 0)
    full = lambda a: pl.BlockSpec(a.shape, lambda i: (0,) * a.ndim)
    return pl.pallas_call(
        _sgu_kernel,
        grid=(n // tm,),
        in_specs=[pl.BlockSpec((tm, d), lambda i: (i, 0)), full(wu), full(wv), full(g), full(b),
                  pl.BlockSpec((1, SG_GROUPS, tm, tm), which), pl.BlockSpec((1, SG_GROUPS, tm, SG_CH), which)],
        out_specs=[pl.BlockSpec((tm, SG_WIDTH), lambda i: (i, 0)), pl.BlockSpec((tm, SG_WIDTH), lambda i: (i, 0))],
        out_shape=[jax.ShapeDtypeStruct((n, SG_WIDTH), BF16), jax.ShapeDtypeStruct((n, SG_WIDTH), F32)],
        compiler_params=_params("parallel"),
        name="sgu_mixer",
    )(x, wu, wv, g, b, ws, bs)


def _sgu_token_mixers(w_s, b_s, chunk, tm):
    w = w_s[:, :chunk, :chunk] * jnp.tril(jnp.ones((chunk, chunk), w_s.dtype))
    rep = tm // chunk
    eye = jnp.eye(rep, dtype=w.dtype)
    wbd = (w[:, None, :, None, :] * eye[None, :, None, :, None]).reshape(SG_GROUPS, tm, tm)
    bias = jnp.broadcast_to(jnp.tile(b_s[:, :chunk], (1, rep))[:, :, None], (SG_GROUPS, tm, SG_CH))
    return wbd, bias


def _rope_tables(positions):
    half = QK_ROPE // 2
    inv = ROPE_THETA ** (-jnp.arange(half, dtype=F32) / half)
    ang = positions.astype(F32)[:, None] * inv[None, :]
    cos, sin = jnp.cos(ang), jnp.sin(ang)
    n = positions.shape[0]
    ones = jnp.ones((n, QK_NOPE), F32)
    zeros_n = jnp.zeros((n, QK_NOPE), F32)
    pad = jnp.zeros((n, HEAD_PAD - QK_NOPE - QK_ROPE), F32)
    cos_t = jnp.concatenate([ones, cos, cos, pad], axis=1)
    sin_t = jnp.concatenate([zeros_n, -sin, sin, pad], axis=1)
    return cos_t, sin_t


def _swap_halves(w):
    half = QK_ROPE // 2
    return jnp.concatenate([w[..., half:], w[..., :half]], axis=-1)


def _head_slots(nope, rope):
    r = (nope if nope is not None else rope).shape[0]
    parts = [nope if nope is not None else jnp.zeros((r, MLA_HEADS, QK_NOPE), F32),
             rope if rope is not None else jnp.zeros((r, MLA_HEADS, QK_ROPE), F32),
             jnp.zeros((r, MLA_HEADS, HEAD_PAD - QK_NOPE - QK_ROPE), F32)]
    return jnp.concatenate(parts, axis=-1).reshape(r, MLA_HEADS * HEAD_PAD)


def _rope_slot(w):
    r = w.shape[0]
    return jnp.concatenate([jnp.zeros((r, QK_NOPE), w.dtype), w,
                            jnp.zeros((r, HEAD_PAD - QK_NOPE - QK_ROPE), w.dtype)], axis=1)


TM_FFN = 512
TF_FFN = 512
TM_PROJ = 512
TM_OUT = 512
TM_MLA = 256
TM_SGU = 256
T_POOL = 512
T_SSM = 512
TQ_ATTN = 256
TK_ATTN = 512
TK_SAMPLE = 1024
SB_POOL = 8


def kernel(x_prompt, x_sample, cache_pool, state_ssm_re, state_ssm_im, cache_ckv, cache_kpe, ln_g, ln_b, ffn1_w1, ffn1_w3, ffn1_w2, ffn2_w1, ffn2_w3, ffn2_w2, w_in_e, pool_w, pool_scale, ssm_lam_re, ssm_lam_im, ssm_log_dt, ssm_b_re, ssm_b_im, ssm_c_re, ssm_c_im, ssm_d, ssm_w_glu, ssm_b_glu, w_out_e, w_in_o, mla_g_q, mla_g_kv, mla_w_uq, mla_w_uk, mla_w_uv, sg_g_v, sg_b_v, sg_w_s, sg_b_s, w_out_o):
    nbp, sp, d = x_prompt.shape
    nbs, ss, _ = x_sample.shape
    past = cache_ckv.shape[2]
    n_p, n_s = nbp * sp, nbs * ss
    x = jnp.concatenate([x_prompt.reshape(n_p, d), x_sample.reshape(n_s, d)], axis=0)
    bf = lambda a: a.astype(BF16)
    row = lambda a: a.reshape(1, -1)

    def ffn(x, w1, w3, w2, layer, k):
        return _ffn(x, bf(w1[layer]), bf(w3[layer]), bf(w2[layer]), row(ln_g[layer, k]), row(ln_b[layer, k]),
                    tm=min(TM_FFN, n_s), tf=TF_FFN)

    x = ffn(x, ffn1_w1, ffn1_w3, ffn1_w2, 0, 0)
    z = _proj(x, bf(w_in_e[0]), tm=min(TM_PROJ, n_s))
    z_p = z[:n_p].reshape(nbp, sp, d)
    z_s = z[n_p:].reshape(nbs, ss, d)
    pw = bf(pool_w[0])
    psc = pool_scale[0].reshape(len(POOL_WINDOWS), 1, POOL_CH)
    a_p = _pool(z_p, z_p, pw, psc, sb=1, t=min(T_POOL, sp), pos0=0, zero_first=True)
    hist = jnp.pad(cache_pool[0], ((0, 0), (POOL_HALO - POOL_HIST, 0), (0, 0)))
    sbp = math.gcd(SB_POOL, nbs)
    a_s = _pool(z_s, hist, pw, psc, sb=sbp, t=ss, pos0=past, zero_first=False)

    ab_re, ab_im, bbt_re, bbt_im = _ssm_prep(ssm_lam_re[0], ssm_lam_im[0], ssm_log_dt[0], ssm_b_re[0], ssm_b_im[0])
    bd_b_re = bf(_block_diag(bbt_re))
    bd_b_im = bf(_block_diag(bbt_im))
    bd_c_re = bf(_block_diag(ssm_c_re[0].transpose(0, 2, 1)))
    bd_c_im = bf(_block_diag(ssm_c_im[0].transpose(0, 2, 1)))
    a_re = ab_re.reshape(1, SSM_LANES)
    a_im = ab_im.reshape(1, SSM_LANES)
    ssm_args = (bd_b_re, bd_b_im, a_re, a_im, bd_c_re, bd_c_im, row(ssm_d[0]), bf(ssm_w_glu[0]), row(ssm_b_glu[0]))
    zero_state = jnp.zeros((nbp, 1, SSM_LANES), F32)
    b_p, sre_p, sim_p = _ssm(z_p, zero_state, zero_state, *ssm_args, t=min(T_SSM, sp))
    b_s, sre_s, sim_s = _ssm(z_s, state_ssm_re[0].reshape(nbs, 1, SSM_LANES),
                             state_ssm_im[0].reshape(nbs, 1, SSM_LANES), *ssm_args, t=ss)
    a_all = jnp.concatenate([a_p.reshape(n_p, POOL_WIDTH), a_s.reshape(n_s, POOL_WIDTH)], axis=0)
    b_all = jnp.concatenate([b_p.reshape(n_p, SSM_WIDTH), b_s.reshape(n_s, SSM_WIDTH)], axis=0)
    woe = bf(w_out_e[0])
    x = _out_ln(x, a_all, b_all, woe[:POOL_WIDTH], woe[POOL_WIDTH:], row(ln_g[0, 1]), row(ln_b[0, 1]),
                tm=min(TM_OUT, n_s))
    x = ffn(x, ffn2_w1, ffn2_w3, ffn2_w2, 0, 2)

    pool_p = z_p[:, sp - POOL_HIST:, :POOL_WIDTH][None]
    pool_s = z_s[:, ss - POOL_HIST:, :POOL_WIDTH][None]
    state_shape = lambda nb: (1, nb, SSM_GROUPS, SSM_STATE)

    x = ffn(x, ffn1_w1, ffn1_w3, ffn1_w2, 1, 0)
    wio = w_in_o[0]
    w_kpe = wio[:, Q_LORA + KV_LORA:MLA_IN]
    wuq = mla_w_uq[0]
    wq = _head_slots(wuq[:, :, :QK_NOPE], wuq[:, :, QK_NOPE:])
    wqs = _head_slots(None, _swap_halves(wuq[:, :, QK_NOPE:]))
    wk = _head_slots(mla_w_uk[0], None)
    wv = mla_w_uv[0].reshape(KV_LORA, MLA_WIDTH)
    positions = jnp.concatenate([jnp.tile(jnp.arange(sp, dtype=jnp.int32), nbp),
                                 jnp.tile(past + jnp.arange(ss, dtype=jnp.int32), nbs)])
    cos_t, sin_t = _rope_tables(positions)
    q_all, k_all, v_all, ckv_all, kpe_all = _mla_in(
        x, cos_t, sin_t, bf(wio[:, :Q_LORA]), bf(wio[:, Q_LORA:Q_LORA + KV_LORA]), bf(_rope_slot(w_kpe)),
        bf(_rope_slot(_swap_halves(w_kpe))), row(mla_g_q[0]), row(mla_g_kv[0]), bf(wq), bf(wqs), bf(wk), bf(wv),
        tm=min(TM_MLA, n_s))
    hp = MLA_HEADS * HEAD_PAD
    att_p = _flash(q_all[:n_p].reshape(nbp, sp, hp).transpose(0, 2, 1), k_all[:n_p].reshape(nbp, sp, hp),
                   v_all[:n_p].reshape(nbp, sp, MLA_WIDTH).transpose(0, 2, 1),
                   tq=min(TQ_ATTN, sp), tk=min(TK_ATTN, sp)).transpose(0, 2, 1)
    ckv_s = ckv_all[n_p:].reshape(nbs, ss, KV_LORA)
    kpe_s = kpe_all[n_p:].reshape(nbs, ss, HEAD_PAD)
    cache_kpe_pad = bf(jnp.pad(cache_kpe[0], ((0, 0), (0, 0), (QK_NOPE, HEAD_PAD - QK_NOPE - QK_ROPE))))
    wuk_pad = bf(jnp.pad(mla_w_uk[0].transpose(1, 2, 0), ((0, 0), (0, HEAD_PAD - QK_NOPE), (0, 0))))
    wuv_h = mla_w_uv[0].transpose(1, 0, 2)
    odd = (jnp.arange(MLA_HEADS) % 2 == 1)[:, None, None]
    wuv_pad = bf(jnp.where(odd, jnp.pad(wuv_h, ((0, 0), (0, 0), (V_HEAD, 0))),
                           jnp.pad(wuv_h, ((0, 0), (0, 0), (0, V_HEAD)))))
    att_s = _attend_sample(q_all[n_p:].reshape(nbs, ss, hp), cache_ckv[0], cache_kpe_pad, ckv_s, kpe_s,
                           wuk_pad, wuv_pad, tk=min(TK_SAMPLE, past))
    att = jnp.concatenate([att_p.reshape(n_p, MLA_WIDTH), att_s.reshape(n_s, MLA_WIDTH)], axis=0)

    tm_sgu = min(TM_SGU, n_s)
    ws_p, bs_p = _sgu_token_mixers(sg_w_s[0], sg_b_s[0], min(sp, SG_CHUNK), tm_sgu)
    ws_s, bs_s = _sgu_token_mixers(sg_w_s[0], sg_b_s[0], min(ss, SG_CHUNK), tm_sgu)
    sg_out, sgv_all = _sgu(x, bf(wio[:, MLA_IN:MLA_IN + SG_WIDTH]), bf(wio[:, MLA_IN + SG_WIDTH:]),
                           row(sg_g_v[0]), row(sg_b_v[0]), bf(jnp.stack([ws_p, ws_s])),
                           jnp.stack([bs_p, bs_s]), tm=tm_sgu, n_first=n_p)
    woo = bf(w_out_o[0])
    x = _out_ln(x, att, sg_out, woo[:MLA_WIDTH], woo[MLA_WIDTH:], row(ln_g[1, 1]), row(ln_b[1, 1]),
                tm=min(TM_OUT, n_s))
    x = ffn(x, ffn2_w1, ffn2_w3, ffn2_w2, 1, 2)

    rope_lanes = slice(QK_NOPE, QK_NOPE + QK_ROPE)
    return (x[:n_p].reshape(nbp, sp, d), x[n_p:].reshape(nbs, ss, d),
            pool_p, sre_p.reshape(state_shape(nbp)), sim_p.reshape(state_shape(nbp)),
            ckv_all[:n_p].reshape(1, nbp, sp, KV_LORA), kpe_all[:n_p, rope_lanes].reshape(1, nbp, sp, QK_ROPE),
            pool_s, sre_s.reshape(state_shape(nbs)), sim_s.reshape(state_shape(nbs)),
            ckv_s[None], kpe_s[:, :, rope_lanes][None],
            sgv_all[n_p:].reshape(1, nbs, ss, SG_WIDTH))
```

```python
import functools
import math

import jax
import jax.numpy as jnp
from jax import lax
from jax.experimental import pallas as pl
from jax.experimental.pallas import tpu as pltpu

F32 = jnp.float32
BF16 = jnp.bfloat16

D_MODEL = 2048
DEPTH = 2
CHUNK = 64
D_FF = 5632
ALPHA = (2 * DEPTH) ** 0.25
LN_EPS = 1e-5
RMS_EPS = 1e-6
POOL_WINDOWS = (2, 4, 8, 16)
POOL_CH = 384
POOL_WIDTH = len(POOL_WINDOWS) * POOL_CH
POOL_HIST = max(POOL_WINDOWS) - 1
POOL_HALO = 16
SSM_WIDTH = D_MODEL - POOL_WIDTH
SSM_GROUP_CH = 16
SSM_GROUPS = SSM_WIDTH // SSM_GROUP_CH
SSM_STATE = 64
SSM_LANES = SSM_GROUPS * SSM_STATE
MLA_HEADS = 16
Q_LORA = 512
KV_LORA = 256
QK_NOPE = 64
QK_ROPE = 32
V_HEAD = 64
HEAD_PAD = 128
MLA_WIDTH = MLA_HEADS * V_HEAD
MLA_IN = Q_LORA + KV_LORA + QK_ROPE
ATTN_SCALE = (QK_NOPE + QK_ROPE) ** -0.5
Q_SCALE = ATTN_SCALE * math.log2(math.e)
ROPE_THETA = 10000.0
SG_CHUNK = 128
SG_GROUPS = 8
SG_WIDTH = D_MODEL - MLA_WIDTH
SG_CH = SG_WIDTH // SG_GROUPS
NEG_INF = -1e30

VMEM_LIMIT_BYTES = 56 * 1024 * 1024


def _params(*semantics):
    return pltpu.CompilerParams(dimension_semantics=semantics, vmem_limit_bytes=VMEM_LIMIT_BYTES)


def _dot(a, b):
    return jnp.dot(a, b, preferred_element_type=F32)


def _dot_nt(a, b):
    return lax.dot_general(a, b, (((1,), (1,)), ((), ())), preferred_element_type=F32)


def _layer_norm(y, g, b):
    mu = jnp.mean(y, axis=-1, keepdims=True)
    d = y - mu
    var = jnp.mean(d * d, axis=-1, keepdims=True)
    return d * lax.rsqrt(var + LN_EPS) * g + b


def _rms_norm(y, g):
    return y * lax.rsqrt(jnp.mean(y * y, axis=-1, keepdims=True) + RMS_EPS) * g


def _sigmoid(y):
    return 1.0 / (1.0 + jnp.exp(-y))


def _gelu_tanh(y):
    c = math.sqrt(2.0 / math.pi)
    return 0.5 * y * (1.0 + jnp.tanh(c * (y + 0.044715 * (y * y * y))))


def _ffn_kernel(x_ref, w1_ref, w3_ref, w2_ref, g_ref, b_ref, o_ref, xb_ref, h_ref, r1a_ref, r3a_ref,
                r1b_ref, r3b_ref, *, nf, nn, tf, tn):
    j = pl.program_id(1)
    slots = ((r1a_ref, r3a_ref), (r1b_ref, r3b_ref))

    def project(slot):
        xb = xb_ref[...]
        slots[slot][0][...] = _dot(xb, w1_ref[...])
        slots[slot][1][...] = _dot(xb, w3_ref[...])

    def gate(slot, f):
        h1 = slots[slot][0][...]
        h_ref[f] = (h1 * _sigmoid(h1) * slots[slot][1][...]).astype(BF16)

    @pl.when(j == 0)
    def _():
        xb_ref[...] = x_ref[...].astype(BF16)
        project(0)

    for parity in range(2):
        @pl.when(jnp.logical_and(jnp.logical_and(j > 0, j < nf), j % 2 == parity))
        def _(parity=parity):
            gate(1 - parity, j - 1)
            project(parity)

    for n in range(nn):
        @pl.when(j == nf + n)
        def _(n=n):
            if n == 0:
                gate((nf - 1) % 2, nf - 1)
            acc = _dot(h_ref[0], w2_ref[0:tf, :])
            for f in range(1, nf):
                acc += _dot(h_ref[f], w2_ref[f * tf:(f + 1) * tf, :])
            o_ref[:, n * tn:(n + 1) * tn] = acc

    @pl.when(j == nf + nn - 1)
    def _():
        y = ALPHA * x_ref[...] + 0.5 * o_ref[...]
        o_ref[...] = _layer_norm(y, g_ref[...], b_ref[...])


def _ffn(x, w1, w3, w2, g, b, *, tm, tf, tn, row0=0, rows=None):
    d = x.shape[1]
    rows = x.shape[0] if rows is None else rows
    f_all = w1.shape[1]
    nf, nn, t0 = f_all // tf, d // tn, row0 // tm
    return pl.pallas_call(
        functools.partial(_ffn_kernel, nf=nf, nn=nn, tf=tf, tn=tn),
        grid=(rows // tm, nf + nn),
        in_specs=[
            pl.BlockSpec((tm, d), lambda i, j: (i + t0, 0)),
            pl.BlockSpec((d, tf), lambda i, j: (0, jnp.minimum(j, nf - 1))),
            pl.BlockSpec((d, tf), lambda i, j: (0, jnp.minimum(j, nf - 1))),
            pl.BlockSpec((f_all, tn), lambda i, j: (0, jnp.clip(j - nf, 0, nn - 1))),
            pl.BlockSpec((1, d), lambda i, j: (0, 0)),
            pl.BlockSpec((1, d), lambda i, j: (0, 0)),
        ],
        out_specs=pl.BlockSpec((tm, d), lambda i, j: (i, 0)),
        out_shape=jax.ShapeDtypeStruct((rows, d), F32),
        scratch_shapes=[pltpu.VMEM((tm, d), BF16), pltpu.VMEM((nf, tm, tf), BF16),
                        pltpu.VMEM((tm, tf), F32), pltpu.VMEM((tm, tf), F32),
                        pltpu.VMEM((tm, tf), F32), pltpu.VMEM((tm, tf), F32)],
        compiler_params=_params("parallel", "arbitrary"),
        name="ffn_ln",
    )(x, w1, w3, w2, g, b)


def _proj_kernel(x_ref, w_ref, o_ref):
    o_ref[...] = _dot(x_ref[...].astype(BF16), w_ref[...])


def _proj(x, w, *, tm):
    n, d = x.shape
    m = w.shape[1]
    return pl.pallas_call(
        _proj_kernel,
        grid=(n // tm,),
        in_specs=[pl.BlockSpec((tm, d), lambda i: (i, 0)), pl.BlockSpec((d, m), lambda i: (0, 0))],
        out_specs=pl.BlockSpec((tm, m), lambda i: (i, 0)),
        out_shape=jax.ShapeDtypeStruct((n, m), F32),
        compiler_params=_params("parallel"),
        name="proj_in",
    )(x, w)


def _out_ln_kernel(x_ref, a_ref, b_ref, wa_ref, wb_ref, g_ref, beta_ref, o_ref):
    mix = _dot(a_ref[...], wa_ref[...]) + _dot(b_ref[...], wb_ref[...])
    o_ref[...] = _layer_norm(ALPHA * x_ref[...] + mix, g_ref[...], beta_ref[...])


def _out_ln(x, a, b, wa, wb, g, beta, *, tm):
    n, d = x.shape
    ka, kb = a.shape[1], b.shape[1]
    return pl.pallas_call(
        _out_ln_kernel,
        grid=(n // tm,),
        in_specs=[
            pl.BlockSpec((tm, d), lambda i: (i, 0)),
            pl.BlockSpec((tm, ka), lambda i: (i, 0)),
            pl.BlockSpec((tm, kb), lambda i: (i, 0)),
            pl.BlockSpec((ka, d), lambda i: (0, 0)),
            pl.BlockSpec((kb, d), lambda i: (0, 0)),
            pl.BlockSpec((1, d), lambda i: (0, 0)),
            pl.BlockSpec((1, d), lambda i: (0, 0)),
        ],
        out_specs=pl.BlockSpec((tm, d), lambda i: (i, 0)),
        out_shape=jax.ShapeDtypeStruct((n, d), F32),
        compiler_params=_params("parallel"),
        name="out_ln",
    )(x, a, b, wa, wb, g, beta)


def _pool_kernel(u_ref, halo_ref, w_ref, sc_ref, *rest, sb, t, pos0, zero_first):
    o_ref, buf_ref = rest[-2:]
    ti = pl.program_id(1)
    pos = lax.broadcasted_iota(jnp.int32, (t, 1), 0) + (ti * t + pos0 + 1)
    for s in range(sb):
        if zero_first:
            halo = jnp.where(ti == 0, 0.0, halo_ref[...])
        else:
            halo = halo_ref[s]
        buf_ref[0:POOL_HALO, :] = halo
        buf_ref[POOL_HALO:POOL_HALO + t, :] = u_ref[s * t:(s + 1) * t, :]
        for g, w in enumerate(POOL_WINDOWS):
            sl = slice(g * POOL_CH, (g + 1) * POOL_CH)
            u = buf_ref[POOL_HALO:POOL_HALO + t, sl]
            acc = u
            for k in range(1, w):
                acc = acc + buf_ref[POOL_HALO - k:POOL_HALO - k + t, sl]
            cnt = jnp.minimum(pos, w).astype(F32)
            d = (acc / cnt - u).astype(BF16)
            o_ref[s * t:(s + 1) * t, sl] = (_dot(d, w_ref[g]) * sc_ref[g]).astype(o_ref.dtype)


def _seq_rows(row0, s_len, rows, t):
    base, per_seq = row0 // rows, s_len // t
    return lambda i, j: base + i * per_seq + j


def _into(prev):
    if prev is None:
        return [], [], {}
    return [pl.BlockSpec(memory_space=pl.ANY)], [prev], None


def _pool(z, halo_src, w, sc, prev, *, row0, n_seq, s_len, sb, t, pos0, zero_first):
    rows = sb * t
    blk = _seq_rows(row0, s_len, rows, t)
    if zero_first:
        hb = t // POOL_HALO
        halo_spec = pl.BlockSpec((POOL_HALO, POOL_WIDTH), lambda i, j: (jnp.maximum(blk(i, j) * hb - 1, 0), 0))
    else:
        halo_spec = pl.BlockSpec((sb, POOL_HALO, POOL_WIDTH), lambda i, j: (i, 0, 0))
    extra_specs, extra_args, _ = _into(prev)
    return pl.pallas_call(
        functools.partial(_pool_kernel, sb=sb, t=t, pos0=pos0, zero_first=zero_first),
        grid=(n_seq // sb, s_len // t),
        in_specs=[
            pl.BlockSpec((rows, POOL_WIDTH), lambda i, j: (blk(i, j), 0)),
            halo_spec,
            pl.BlockSpec((len(POOL_WINDOWS), POOL_CH, POOL_CH), lambda i, j: (0, 0, 0)),
            pl.BlockSpec((len(POOL_WINDOWS), 1, POOL_CH), lambda i, j: (0, 0, 0)),
        ] + extra_specs,
        out_specs=pl.BlockSpec((rows, POOL_WIDTH), lambda i, j: (blk(i, j), 0)),
        out_shape=jax.ShapeDtypeStruct((z.shape[0], POOL_WIDTH), BF16),
        input_output_aliases={4: 0} if prev is not None else {},
        scratch_shapes=[pltpu.VMEM((POOL_HALO + t, POOL_WIDTH), F32)],
        compiler_params=_params("parallel", "arbitrary"),
        name="pool_mixer",
    )(z, halo_src, w, sc, *extra_args)


def _ssm_prep_kernel(lr_ref, li_ref, ldt_ref, br_ref, bi_ref, abr_ref, abi_ref, bbr_ref, bbi_ref):
    lr = lr_ref[...]
    li = li_ref[...]
    dt = jnp.exp(ldt_ref[...])
    mag = jnp.exp(lr * dt)
    abr = mag * jnp.cos(li * dt)
    abi = mag * jnp.sin(li * dt)
    den = lr * lr + li * li
    nr = abr - 1.0
    cor = (nr * lr + abi * li) / den
    coi = (abi * lr - nr * li) / den
    br = br_ref[...]
    bi = bi_ref[...]
    abr_ref[...] = abr
    abi_ref[...] = abi
    bbr_ref[...] = cor * br - coi * bi
    bbi_ref[...] = cor * bi + coi * br


def _ssm_prep(lam_re, lam_im, log_dt, b_re, b_im):
    g, p = lam_re.shape
    c = b_re.shape[-1]
    lr = lam_re.reshape(g, 1, p)
    li = lam_im.reshape(g, 1, p)
    ldt = jnp.broadcast_to(log_dt.reshape(g, 1, 1), (g, 1, p))
    brt = b_re.transpose(0, 2, 1)
    bit = b_im.transpose(0, 2, 1)
    return pl.pallas_call(
        _ssm_prep_kernel,
        out_shape=[jax.ShapeDtypeStruct((g, 1, p), F32), jax.ShapeDtypeStruct((g, 1, p), F32),
                   jax.ShapeDtypeStruct((g, c, p), F32), jax.ShapeDtypeStruct((g, c, p), F32)],
        name="ssm_discretise",
    )(lr, li, ldt, brt, bit)


SSM_LANE_SPLIT = 2


def _ssm_kernel(u_ref, hre_ref, him_ref, bbr_ref, bbi_ref, ar_ref, ai_ref, cr_ref, ci_ref, d_ref,
                wg_ref, bg_ref, *rest, t, nt):
    o_ref, sre_ref, sim_ref, xr_ref, xi_ref, st_ref = rest[-6:]
    ti = pl.program_id(1)

    @pl.when(ti == 0)
    def _():
        st_ref[0:1, :] = hre_ref[0]
        st_ref[1:2, :] = him_ref[0]

    u = u_ref[...]
    ub = u.astype(BF16)
    xr_ref[...] = _dot(ub, bbr_ref[...])
    xi_ref[...] = _dot(ub, bbi_ref[...])

    slab = SSM_LANES // SSM_LANE_SPLIT
    for c in range(SSM_LANE_SPLIT):
        sl = slice(c * slab, (c + 1) * slab)
        ar = ar_ref[:, sl]
        ai = ai_ref[:, sl]

        def step(i, carry, sl=sl, ar=ar, ai=ai):
            pr, pi = carry
            nr = ar * pr - ai * pi + xr_ref[pl.ds(i, 1), sl]
            ni = ar * pi + ai * pr + xi_ref[pl.ds(i, 1), sl]
            xr_ref[pl.ds(i, 1), sl] = nr
            xi_ref[pl.ds(i, 1), sl] = ni
            return nr, ni

        pr, pi = lax.fori_loop(0, t, step, (st_ref[0:1, sl], st_ref[1:2, sl]))
        st_ref[0:1, sl] = pr
        st_ref[1:2, sl] = pi

    y = (_dot(xr_ref[...].astype(BF16), cr_ref[...]) - _dot(xi_ref[...].astype(BF16), ci_ref[...])
         + d_ref[...] * u)
    gl = _gelu_tanh(y)
    o_ref[...] = (gl * _sigmoid(_dot(gl.astype(BF16), wg_ref[...]) + bg_ref[...])).astype(o_ref.dtype)

    @pl.when(ti == nt - 1)
    def _():
        sre_ref[0] = st_ref[0:1, :]
        sim_ref[0] = st_ref[1:2, :]


def _ssm(z, h_re, h_im, bd_b_re, bd_b_im, a_re, a_im, bd_c_re, bd_c_im, d_skip, w_glu, b_glu, prev,
         *, row0, n_seq, s_len, t):
    nt = s_len // t
    col = POOL_WIDTH // SSM_WIDTH
    blk = _seq_rows(row0, s_len, t, t)
    const2 = lambda i, j: (0, 0)
    state_spec = pl.BlockSpec((1, 1, SSM_LANES), lambda i, j: (i, 0, 0))
    extra_specs, extra_args, _ = _into(prev)
    return pl.pallas_call(
        functools.partial(_ssm_kernel, t=t, nt=nt),
        grid=(n_seq, nt),
        in_specs=[
            pl.BlockSpec((t, SSM_WIDTH), lambda i, j: (blk(i, j), col)),
            state_spec, state_spec,
            pl.BlockSpec((SSM_WIDTH, SSM_LANES), const2),
            pl.BlockSpec((SSM_WIDTH, SSM_LANES), const2),
            pl.BlockSpec((1, SSM_LANES), const2),
            pl.BlockSpec((1, SSM_LANES), const2),
            pl.BlockSpec((SSM_LANES, SSM_WIDTH), const2),
            pl.BlockSpec((SSM_LANES, SSM_WIDTH), const2),
            pl.BlockSpec((1, SSM_WIDTH), const2),
            pl.BlockSpec((SSM_WIDTH, SSM_WIDTH), const2),
            pl.BlockSpec((1, SSM_WIDTH), const2),
        ] + extra_specs,
        out_specs=[pl.BlockSpec((t, SSM_WIDTH), lambda i, j: (blk(i, j), 0)), state_spec, state_spec],
        out_shape=[jax.ShapeDtypeStruct((z.shape[0], SSM_WIDTH), BF16),
                   jax.ShapeDtypeStruct((n_seq, 1, SSM_LANES), F32),
                   jax.ShapeDtypeStruct((n_seq, 1, SSM_LANES), F32)],
        input_output_aliases={12: 0} if prev is not None else {},
        scratch_shapes=[pltpu.VMEM((t, SSM_LANES), F32), pltpu.VMEM((t, SSM_LANES), F32),
                        pltpu.VMEM((8, SSM_LANES), F32)],
        compiler_params=_params("parallel", "arbitrary"),
        name="s5_mixer",
    )(z, h_re, h_im, bd_b_re, bd_b_im, a_re, a_im, bd_c_re, bd_c_im, d_skip, w_glu, b_glu, *extra_args)


def _block_diag(blocks):
    g, r, c = blocks.shape
    eye = jnp.eye(g, dtype=blocks.dtype)
    return (blocks[:, :, None, :] * eye[:, None, :, None]).reshape(g * r, g * c)


def _mla_in_kernel(x_ref, cos_ref, sin_ref, wcq_ref, wckv_ref, wkpe_ref, wkpes_ref, gq_ref, gkv_ref,
                   wq_ref, wqs_ref, wk_ref, wv_ref, q_ref, k_ref, v_ref, ckv_ref, kpe_ref):
    xb = x_ref[...].astype(BF16)
    cos = cos_ref[...]
    sin = sin_ref[...]
    cqn = _rms_norm(_dot(xb, wcq_ref[...]), gq_ref[...]).astype(BF16)
    ckv = _rms_norm(_dot(xb, wckv_ref[...]), gkv_ref[...])
    ckv_ref[...] = ckv
    kpe = _dot(xb, wkpe_ref[...]) * cos + _dot(xb, wkpes_ref[...]) * sin
    kpe_ref[...] = kpe
    qa = _dot(cqn, wq_ref[...])
    qb = _dot(cqn, wqs_ref[...])
    cb = ckv.astype(BF16)
    kn = _dot(cb, wk_ref[...])
    for h in range(MLA_HEADS):
        sl = slice(h * HEAD_PAD, (h + 1) * HEAD_PAD)
        q_ref[:, sl] = ((qa[:, sl] * cos + qb[:, sl] * sin) * Q_SCALE).astype(BF16)
        k_ref[:, sl] = (kn[:, sl] + kpe).astype(BF16)
    v_ref[...] = _dot(cb, wv_ref[...]).astype(BF16)


def _mla_in(x, cos, sin, wcq, wckv, wkpe, wkpes, gq, gkv, wq, wqs, wk, wv, *, tm):
    n, d = x.shape
    hp = MLA_HEADS * HEAD_PAD
    row = lambda w: pl.BlockSpec((tm, w), lambda i: (i, 0))
    full = lambda a: pl.BlockSpec(a.shape, lambda i: (0,) * a.ndim)
    return pl.pallas_call(
        _mla_in_kernel,
        grid=(n // tm,),
        in_specs=[row(d), row(HEAD_PAD), row(HEAD_PAD), full(wcq), full(wckv), full(wkpe), full(wkpes),
                  full(gq), full(gkv), full(wq), full(wqs), full(wk), full(wv)],
        out_specs=[row(hp), row(hp), row(MLA_WIDTH), row(KV_LORA), row(HEAD_PAD)],
        out_shape=[jax.ShapeDtypeStruct((n, hp), BF16), jax.ShapeDtypeStruct((n, hp), BF16),
                   jax.ShapeDtypeStruct((n, MLA_WIDTH), BF16), jax.ShapeDtypeStruct((n, KV_LORA), F32),
                   jax.ShapeDtypeStruct((n, HEAD_PAD), F32)],
        compiler_params=_params("parallel"),
        name="mla_project",
    )(x, cos, sin, wcq, wckv, wkpe, wkpes, gq, gkv, wq, wqs, wk, wv)


def _flash_kernel(qt_ref, kt_ref, q_ref, k_ref, v_ref, o_ref, m_ref, l_ref, acc_ref, *, tq, tk):
    t = pl.program_id(1)
    qi = qt_ref[t]
    ki = kt_ref[t]
    q0 = qi * tq
    k0 = ki * tk

    @pl.when(ki == 0)
    def _():
        m_ref[...] = jnp.full(m_ref.shape, NEG_INF, F32)
        l_ref[...] = jnp.zeros(l_ref.shape, F32)
        acc_ref[...] = jnp.zeros(acc_ref.shape, F32)

    def tile(masked):
        if masked:
            k_chunk = (lax.broadcasted_iota(jnp.int32, (tk, tq), 0) + k0) // CHUNK
            q_chunk = (lax.broadcasted_iota(jnp.int32, (tk, tq), 1) + q0) // CHUNK
            mask = k_chunk <= q_chunk
        def scores(h):
            return _dot(k_ref[0, :, h * HEAD_PAD:(h + 1) * HEAD_PAD], q_ref[0, h * HEAD_PAD:(h + 1) * HEAD_PAD, :])

        s_next = scores(0)
        for h in range(MLA_HEADS):
            s = s_next
            if h + 1 < MLA_HEADS:
                s_next = scores(h + 1)
            if masked:
                s = jnp.where(mask, s, NEG_INF)
            s3 = s.reshape(tk // 8, 8, tq)
            m_prev = m_ref[h]
            m_new = jnp.maximum(m_prev, jnp.max(jnp.max(s3, axis=0), axis=0, keepdims=True))
            alpha = jnp.exp2(m_prev - m_new)
            p3 = jnp.exp2(s3 - m_new[None])
            l_ref[h] = alpha * l_ref[h] + jnp.sum(p3, axis=0)
            m_ref[h] = m_new
            pv = _dot(v_ref[0, h * V_HEAD:(h + 1) * V_HEAD, :], p3.reshape(tk, tq).astype(BF16))
            rows = slice(h * V_HEAD, (h + 1) * V_HEAD)
            acc = acc_ref[rows, :].reshape(V_HEAD // 8, 8, tq) * alpha[None] + pv.reshape(V_HEAD // 8, 8, tq)
            acc_ref[rows, :] = acc.reshape(V_HEAD, tq)

    interior = k0 + tk <= q0 + CHUNK

    @pl.when(interior)
    def _():
        tile(False)

    @pl.when(jnp.logical_not(interior))
    def _():
        tile(True)

    @pl.when(k0 + tk >= q0 + tq)
    def _():
        for h in range(MLA_HEADS):
            rows = slice(h * V_HEAD, (h + 1) * V_HEAD)
            l_row = jnp.sum(l_ref[h], axis=0, keepdims=True)
            o_ref[0, rows, :] = (acc_ref[rows, :] / l_row).astype(o_ref.dtype)


def _flash(q_t, k, v_t, *, tq, tk):
    nb, hp, s_len = q_t.shape
    pairs = [(i, j) for i in range(s_len // tq) for j in range(s_len // tk) if j * tk < (i + 1) * tq]
    q_tab = jnp.asarray([p[0] for p in pairs], jnp.int32)
    k_tab = jnp.asarray([p[1] for p in pairs], jnp.int32)
    grid_spec = pltpu.PrefetchScalarGridSpec(
        num_scalar_prefetch=2,
        grid=(nb, len(pairs)),
        in_specs=[
            pl.BlockSpec((1, hp, tq), lambda b, t, qt, kt: (b, 0, qt[t])),
            pl.BlockSpec((1, tk, hp), lambda b, t, qt, kt: (b, kt[t], 0)),
            pl.BlockSpec((1, MLA_WIDTH, tk), lambda b, t, qt, kt: (b, 0, kt[t])),
        ],
        out_specs=pl.BlockSpec((1, MLA_WIDTH, tq), lambda b, t, qt, kt: (b, 0, qt[t])),
        scratch_shapes=[pltpu.VMEM((MLA_HEADS, 8, tq), F32), pltpu.VMEM((MLA_HEADS, 8, tq), F32),
                        pltpu.VMEM((MLA_WIDTH, tq), F32)],
    )
    return pl.pallas_call(
        functools.partial(_flash_kernel, tq=tq, tk=tk),
        grid_spec=grid_spec,
        out_shape=jax.ShapeDtypeStruct((nb, MLA_WIDTH, s_len), BF16),
        compiler_params=_params("parallel", "arbitrary"),
        name="mla_attend_prompt",
    )(q_tab, k_tab, q_t, k, v_t)


def _attend_sample_kernel(q_ref, cc_ref, ck_ref, nc_ref, nk_ref, wuk_ref, wuv_ref, o_ref, q2_ref, ql_ref,
                          *, s, past, tk):
    for h in range(MLA_HEADS):
        qh = q_ref[0, :, h * HEAD_PAD:(h + 1) * HEAD_PAD]
        q2_ref[h * s:(h + 1) * s, :] = qh
        ql_ref[h * s:(h + 1) * s, :] = _dot(qh, wuk_ref[h]).astype(BF16)
    q2 = q2_ref[...]
    ql = ql_ref[...]
    rows = MLA_HEADS * s
    m = jnp.full((rows, 1), NEG_INF, F32)
    l = jnp.zeros((rows, 1), F32)
    acc = jnp.zeros((rows, KV_LORA), F32)
    tiles = [(cc_ref, ck_ref, j * tk, tk) for j in range(past // tk)] + [(nc_ref, nk_ref, 0, s)]
    for c_ref, kp_ref, start, size in tiles:
        c = c_ref[0, start:start + size, :].astype(BF16)
        kp = kp_ref[0, start:start + size, :].astype(BF16)
        sc = _dot_nt(ql, c) + _dot_nt(q2, kp)
        m_new = jnp.maximum(m, jnp.max(sc, axis=1, keepdims=True))
        alpha = jnp.exp2(m - m_new)
        pm = jnp.exp2(sc - m_new)
        l = alpha * l + jnp.sum(pm, axis=1, keepdims=True)
        acc = alpha * acc + _dot(pm.astype(BF16), c)
        m = m_new
    olat = (acc / l).astype(BF16)
    for p in range(MLA_HEADS // 2):
        he, ho = 2 * p, 2 * p + 1
        o_ref[0, :, p * HEAD_PAD:(p + 1) * HEAD_PAD] = (
            _dot(olat[he * s:(he + 1) * s, :], wuv_ref[he]) + _dot(olat[ho * s:(ho + 1) * s, :], wuv_ref[ho])
        ).astype(o_ref.dtype)


def _attend_sample(q, cache_ckv, cache_kpe_pad, new_ckv, new_kpe_pad, wuk_pad, wuv_pad, *, tk):
    nb, s, hp = q.shape
    past = cache_ckv.shape[1]
    return pl.pallas_call(
        functools.partial(_attend_sample_kernel, s=s, past=past, tk=tk),
        grid=(nb,),
        in_specs=[
            pl.BlockSpec((1, s, hp), lambda b: (b, 0, 0)),
            pl.BlockSpec((1, past, KV_LORA), lambda b: (b, 0, 0)),
            pl.BlockSpec((1, past, HEAD_PAD), lambda b: (b, 0, 0)),
            pl.BlockSpec((1, s, KV_LORA), lambda b: (b, 0, 0)),
            pl.BlockSpec((1, s, HEAD_PAD), lambda b: (b, 0, 0)),
            pl.BlockSpec(wuk_pad.shape, lambda b: (0, 0, 0)),
            pl.BlockSpec(wuv_pad.shape, lambda b: (0, 0, 0)),
        ],
        out_specs=pl.BlockSpec((1, s, MLA_WIDTH), lambda b: (b, 0, 0)),
        out_shape=jax.ShapeDtypeStruct((nb, s, MLA_WIDTH), BF16),
        scratch_shapes=[pltpu.VMEM((MLA_HEADS * s, HEAD_PAD), BF16), pltpu.VMEM((MLA_HEADS * s, KV_LORA), BF16)],
        compiler_params=_params("parallel"),
        name="mla_attend_sample",
    )(q, cache_ckv, cache_kpe_pad, new_ckv, new_kpe_pad, wuk_pad, wuv_pad)


def _sgu_kernel(x_ref, wu_ref, wv_ref, g_ref, b_ref, ws_ref, bs_ref, o_ref, vn_ref):
    xb = x_ref[...].astype(BF16)
    u = _dot(xb, wu_ref[...])
    vn = _layer_norm(_dot(xb, wv_ref[...]), g_ref[...], b_ref[...])
    vn_ref[...] = vn
    vb = vn.astype(BF16)
    for g in range(SG_GROUPS):
        sl = slice(g * SG_CH, (g + 1) * SG_CH)
        mixed = _dot(ws_ref[0, g], vb[:, sl]) + bs_ref[0, g]
        o_ref[:, sl] = (u[:, sl] * mixed).astype(o_ref.dtype)


def _sgu(x, wu, wv, g, b, ws, bs, *, tm, n_first):
    n, d = x.shape
    first_tiles = n_first // tm
    which = lambda i: (jnp.where(i < first_tiles, 0, 1), 0, 0, 0)
    full = lambda a: pl.BlockSpec(a.shape, lambda i: (0,) * a.ndim)
    return pl.pallas_call(
        _sgu_kernel,
        grid=(n // tm,),
        in_specs=[pl.BlockSpec((tm, d), lambda i: (i, 0)), full(wu), full(wv), full(g), full(b),
                  pl.BlockSpec((1, SG_GROUPS, tm, tm), which), pl.BlockSpec((1, SG_GROUPS, tm, SG_CH), which)],
        out_specs=[pl.BlockSpec((tm, SG_WIDTH), lambda i: (i, 0)), pl.BlockSpec((tm, SG_WIDTH), lambda i: (i, 0))],
        out_shape=[jax.ShapeDtypeStruct((n, SG_WIDTH), BF16), jax.ShapeDtypeStruct((n, SG_WIDTH), F32)],
        compiler_params=_params("parallel"),
        name="sgu_mixer",
    )(x, wu, wv, g, b, ws, bs)


def _sgu_token_mixers(w_s, b_s, chunk, tm):
    w = w_s[:, :chunk, :chunk] * jnp.tril(jnp.ones((chunk, chunk), w_s.dtype))
    rep = tm // chunk
    eye = jnp.eye(rep, dtype=w.dtype)
    wbd = (w[:, None, :, None, :] * eye[None, :, None, :, None]).reshape(SG_GROUPS, tm, tm)
    bias = jnp.broadcast_to(jnp.tile(b_s[:, :chunk], (1, rep))[:, :, None], (SG_GROUPS, tm, SG_CH))
    return wbd, bias


def _rope_tables(positions):
    half = QK_ROPE // 2
    inv = ROPE_THETA ** (-jnp.arange(half, dtype=F32) / half)
    ang = positions.astype(F32)[:, None] * inv[None, :]
    cos, sin = jnp.cos(ang), jnp.sin(ang)
    n = positions.shape[0]
    ones = jnp.ones((n, QK_NOPE), F32)
    zeros_n = jnp.zeros((n, QK_NOPE), F32)
    pad = jnp.zeros((n, HEAD_PAD - QK_NOPE - QK_ROPE), F32)
    cos_t = jnp.concatenate([ones, cos, cos, pad], axis=1)
    sin_t = jnp.concatenate([zeros_n, -sin, sin, pad], axis=1)
    return cos_t, sin_t


def _swap_halves(w):
    half = QK_ROPE // 2
    return jnp.concatenate([w[..., half:], w[..., :half]], axis=-1)


def _head_slots(nope, rope):
    r = (nope if nope is not None else rope).shape[0]
    parts = [nope if nope is not None else jnp.zeros((r, MLA_HEADS, QK_NOPE), F32),
             rope if rope is not None else jnp.zeros((r, MLA_HEADS, QK_ROPE), F32),
             jnp.zeros((r, MLA_HEADS, HEAD_PAD - QK_NOPE - QK_ROPE), F32)]
    return jnp.concatenate(parts, axis=-1).reshape(r, MLA_HEADS * HEAD_PAD)


def _rope_slot(w):
    r = w.shape[0]
    return jnp.concatenate([jnp.zeros((r, QK_NOPE), w.dtype), w,
                            jnp.zeros((r, HEAD_PAD - QK_NOPE - QK_ROPE), w.dtype)], axis=1)


TM_FFN = 512
TF_FFN = 512
TN_FFN = 512
TM_PROJ = 512
TM_OUT = 512
TM_MLA = 256
TM_SGU = 256
T_POOL = 512
T_SSM = 512
TQ_ATTN = 512
TK_ATTN = 1024
TK_SAMPLE = 1024
SB_POOL = 8


def kernel(x_prompt, x_sample, cache_pool, state_ssm_re, state_ssm_im, cache_ckv, cache_kpe, ln_g, ln_b, ffn1_w1, ffn1_w3, ffn1_w2, ffn2_w1, ffn2_w3, ffn2_w2, w_in_e, pool_w, pool_scale, ssm_lam_re, ssm_lam_im, ssm_log_dt, ssm_b_re, ssm_b_im, ssm_c_re, ssm_c_im, ssm_d, ssm_w_glu, ssm_b_glu, w_out_e, w_in_o, mla_g_q, mla_g_kv, mla_w_uq, mla_w_uk, mla_w_uv, sg_g_v, sg_b_v, sg_w_s, sg_b_s, w_out_o):
    nbp, sp, d = x_prompt.shape
    nbs, ss, _ = x_sample.shape
    past = cache_ckv.shape[2]
    n_p, n_s = nbp * sp, nbs * ss
    x = jnp.concatenate([x_prompt.reshape(n_p, d), x_sample.reshape(n_s, d)], axis=0)
    bf = lambda a: a.astype(BF16)
    row = lambda a: a.reshape(1, -1)

    def ffn(x, w1, w3, w2, layer, k, **rows):
        return _ffn(x, bf(w1[layer]), bf(w3[layer]), bf(w2[layer]), row(ln_g[layer, k]), row(ln_b[layer, k]),
                    tm=min(TM_FFN, n_s), tf=TF_FFN, tn=TN_FFN, **rows)

    x = ffn(x, ffn1_w1, ffn1_w3, ffn1_w2, 0, 0)
    z = _proj(x, bf(w_in_e[0]), tm=min(TM_PROJ, n_s))
    pw = bf(pool_w[0])
    psc = pool_scale[0].reshape(len(POOL_WINDOWS), 1, POOL_CH)
    a_all = _pool(z, z, pw, psc, None, row0=0, n_seq=nbp, s_len=sp, sb=1, t=min(T_POOL, sp), pos0=0,
                  zero_first=True)
    hist = jnp.pad(cache_pool[0], ((0, 0), (POOL_HALO - POOL_HIST, 0), (0, 0)))
    a_all = _pool(z, hist, pw, psc, a_all, row0=n_p, n_seq=nbs, s_len=ss, sb=math.gcd(SB_POOL, nbs), t=ss,
                  pos0=past, zero_first=False)

    ab_re, ab_im, bbt_re, bbt_im = _ssm_prep(ssm_lam_re[0], ssm_lam_im[0], ssm_log_dt[0], ssm_b_re[0], ssm_b_im[0])
    bd_b_re = bf(_block_diag(bbt_re))
    bd_b_im = bf(_block_diag(bbt_im))
    bd_c_re = bf(_block_diag(ssm_c_re[0].transpose(0, 2, 1)))
    bd_c_im = bf(_block_diag(ssm_c_im[0].transpose(0, 2, 1)))
    a_re = ab_re.reshape(1, SSM_LANES)
    a_im = ab_im.reshape(1, SSM_LANES)
    ssm_args = (bd_b_re, bd_b_im, a_re, a_im, bd_c_re, bd_c_im, row(ssm_d[0]), bf(ssm_w_glu[0]), row(ssm_b_glu[0]))
    zero_state = jnp.zeros((nbp, 1, SSM_LANES), F32)
    b_all, sre_p, sim_p = _ssm(z, zero_state, zero_state, *ssm_args, None, row0=0, n_seq=nbp, s_len=sp,
                               t=min(T_SSM, sp))
    b_all, sre_s, sim_s = _ssm(z, state_ssm_re[0].reshape(nbs, 1, SSM_LANES),
                               state_ssm_im[0].reshape(nbs, 1, SSM_LANES), *ssm_args, b_all,
                               row0=n_p, n_seq=nbs, s_len=ss, t=ss)
    woe = bf(w_out_e[0])
    x = _out_ln(x, a_all, b_all, woe[:POOL_WIDTH], woe[POOL_WIDTH:], row(ln_g[0, 1]), row(ln_b[0, 1]),
                tm=min(TM_OUT, n_s))
    x = ffn(x, ffn2_w1, ffn2_w3, ffn2_w2, 0, 2)

    z_p = z[:n_p].reshape(nbp, sp, d)
    z_s = z[n_p:].reshape(nbs, ss, d)
    pool_p = z_p[:, sp - POOL_HIST:, :POOL_WIDTH][None]
    pool_s = z_s[:, ss - POOL_HIST:, :POOL_WIDTH][None]
    state_shape = lambda nb: (1, nb, SSM_GROUPS, SSM_STATE)

    x = ffn(x, ffn1_w1, ffn1_w3, ffn1_w2, 1, 0)
    wio = w_in_o[0]
    w_kpe = wio[:, Q_LORA + KV_LORA:MLA_IN]
    wuq = mla_w_uq[0]
    wq = _head_slots(wuq[:, :, :QK_NOPE], wuq[:, :, QK_NOPE:])
    wqs = _head_slots(None, _swap_halves(wuq[:, :, QK_NOPE:]))
    wk = _head_slots(mla_w_uk[0], None)
    wv = mla_w_uv[0].reshape(KV_LORA, MLA_WIDTH)
    positions = jnp.concatenate([jnp.tile(jnp.arange(sp, dtype=jnp.int32), nbp),
                                 jnp.tile(past + jnp.arange(ss, dtype=jnp.int32), nbs)])
    cos_t, sin_t = _rope_tables(positions)
    q_all, k_all, v_all, ckv_all, kpe_all = _mla_in(
        x, cos_t, sin_t, bf(wio[:, :Q_LORA]), bf(wio[:, Q_LORA:Q_LORA + KV_LORA]), bf(_rope_slot(w_kpe)),
        bf(_rope_slot(_swap_halves(w_kpe))), row(mla_g_q[0]), row(mla_g_kv[0]), bf(wq), bf(wqs), bf(wk), bf(wv),
        tm=min(TM_MLA, n_s))
    hp = MLA_HEADS * HEAD_PAD
    att_p = _flash(q_all[:n_p].reshape(nbp, sp, hp).transpose(0, 2, 1), k_all[:n_p].reshape(nbp, sp, hp),
                   v_all[:n_p].reshape(nbp, sp, MLA_WIDTH).transpose(0, 2, 1),
                   tq=min(TQ_ATTN, sp), tk=min(TK_ATTN, sp)).transpose(0, 2, 1)
    ckv_s = ckv_all[n_p:].reshape(nbs, ss, KV_LORA)
    kpe_s = kpe_all[n_p:].reshape(nbs, ss, HEAD_PAD)
    cache_kpe_pad = bf(jnp.pad(cache_kpe[0], ((0, 0), (0, 0), (QK_NOPE, HEAD_PAD - QK_NOPE - QK_ROPE))))
    wuk_pad = bf(jnp.pad(mla_w_uk[0].transpose(1, 2, 0), ((0, 0), (0, HEAD_PAD - QK_NOPE), (0, 0))))
    wuv_h = mla_w_uv[0].transpose(1, 0, 2)
    odd = (jnp.arange(MLA_HEADS) % 2 == 1)[:, None, None]
    wuv_pad = bf(jnp.where(odd, jnp.pad(wuv_h, ((0, 0), (0, 0), (V_HEAD, 0))),
                           jnp.pad(wuv_h, ((0, 0), (0, 0), (0, V_HEAD)))))
    att_s = _attend_sample(q_all[n_p:].reshape(nbs, ss, hp), cache_ckv[0], cache_kpe_pad, ckv_s, kpe_s,
                           wuk_pad, wuv_pad, tk=min(TK_SAMPLE, past))
    att = jnp.concatenate([att_p.reshape(n_p, MLA_WIDTH), att_s.reshape(n_s, MLA_WIDTH)], axis=0)

    tm_sgu = min(TM_SGU, n_s)
    ws_p, bs_p = _sgu_token_mixers(sg_w_s[0], sg_b_s[0], min(sp, SG_CHUNK), tm_sgu)
    ws_s, bs_s = _sgu_token_mixers(sg_w_s[0], sg_b_s[0], min(ss, SG_CHUNK), tm_sgu)
    sg_out, sgv_all = _sgu(x, bf(wio[:, MLA_IN:MLA_IN + SG_WIDTH]), bf(wio[:, MLA_IN + SG_WIDTH:]),
                           row(sg_g_v[0]), row(sg_b_v[0]), bf(jnp.stack([ws_p, ws_s])),
                           jnp.stack([bs_p, bs_s]), tm=tm_sgu, n_first=n_p)
    woo = bf(w_out_o[0])
    x = _out_ln(x, att, sg_out, woo[:MLA_WIDTH], woo[MLA_WIDTH:], row(ln_g[1, 1]), row(ln_b[1, 1]),
                tm=min(TM_OUT, n_s))
    y_p = ffn(x, ffn2_w1, ffn2_w3, ffn2_w2, 1, 2, row0=0, rows=n_p)
    y_s = ffn(x, ffn2_w1, ffn2_w3, ffn2_w2, 1, 2, row0=n_p, rows=n_s)

    rope_lanes = slice(QK_NOPE, QK_NOPE + QK_ROPE)
    return (y_p.reshape(nbp, sp, d), y_s.reshape(nbs, ss, d),
            pool_p, sre_p.reshape(state_shape(nbp)), sim_p.reshape(state_shape(nbp)),
            ckv_all[:n_p].reshape(1, nbp, sp, KV_LORA), kpe_all[:n_p, rope_lanes].reshape(1, nbp, sp, QK_ROPE),
            pool_s, sre_s.reshape(state_shape(nbs)), sim_s.reshape(state_shape(nbs)),
            ckv_s[None], kpe_s[:, :, rope_lanes][None],
            sgv_all[n_p:].reshape(1, nbs, ss, SG_WIDTH))
```
